```python
import math
import jax, jax.numpy as jnp
from jax import lax
import numpy as np

D_MODEL = 2048
BATCH = 8
SEQ = 4096
DEPTH = 4

GRID_W = 64
CTX_LEN = 256
EPS = 1e-6
N_MOD = 6
W_BRANCH = D_MODEL // 2
N_BRANCH = 3
W_LRU = W_BRANCH
LRU_HEADS = 4
LRU_BW = W_LRU // LRU_HEADS
CONV_A = 4
CONV_A_PAD = (1, 2)
LRU_C = 8.0
W_SGU = W_BRANCH
CHUNK = 128
SGU_GROUPS = 8
SGU_GW = W_SGU // SGU_GROUPS
ROWS_PER_CHUNK = CHUNK // GRID_W
W_HY = W_BRANCH
HY_ORDER = 2
CONV_C = 3
CONV_C_PAD = (1, 1)
HY_EMB = 33
HY_BANDS = (HY_EMB - 1) // 2
HY_HID = 64
HY_TARGET = 1e-2
HY_FAST_PCT = 0.3
HY_SLOW_PCT = 1.5
HY_MAX_DECAY = math.log(HY_TARGET) / HY_FAST_PCT
HY_MIN_DECAY = math.log(HY_TARGET) / HY_SLOW_PCT
HY_FILTER_SCALE = 0.05
OFF_A = 0
OFF_B = OFF_A + 2 * W_LRU
OFF_C = OFF_B + 2 * W_SGU
OFF_G = OFF_C + 3 * W_HY
D_IN = OFF_G + N_BRANCH * D_MODEL
N_EXPERTS = 16
N_GROUPS = 4
EXP_PER_GROUP = N_EXPERTS // N_GROUPS
TOP_K = 2
D_EXPERT = D_MODEL // 2

kernel_name = 'hybrid_lru_sgu_hyena_moe_dit'


def rmsnorm(x, g):
    xf = x.astype(jnp.float32)
    y = xf * lax.rsqrt(jnp.mean(xf * xf, axis=-1, keepdims=True) + EPS)
    return (y * g.astype(jnp.float32)).astype(x.dtype)


def modulate(x, g, shift, scale):
    return rmsnorm(x, g) * (1.0 + scale) + shift


def dwconv(x, w, b, pad):
    y = lax.conv_general_dilated(x, w[:, None, :], window_strides=(1,), padding=[pad],
                                 dimension_numbers=('NWC', 'WIO', 'NWC'),
                                 feature_group_count=x.shape[-1])
    return y + b


def rglru_coeffs(u, wa, ba, wx, bx, lam):
    bsz, L, _ = u.shape
    uh = u.reshape(bsz, L, LRU_HEADS, LRU_BW)
    r = jax.nn.sigmoid((jnp.einsum('blhi,hij->blhj', uh, wa).reshape(bsz, L, W_LRU) + ba).astype(jnp.float32))
    i = jax.nn.sigmoid((jnp.einsum('blhi,hij->blhj', uh, wx).reshape(bsz, L, W_LRU) + bx).astype(jnp.float32))
    log_a = -LRU_C * r * jax.nn.softplus(-lam.astype(jnp.float32))
    a = jnp.exp(log_a)
    b = jnp.sqrt(-jnp.expm1(2.0 * log_a)) * (i * u.astype(jnp.float32))
    return a, b


def _lin_combine(left, right):
    a1, b1 = left
    a2, b2 = right
    return a1 * a2, a2 * b1 + b2


def linear_scan(a, b, h0, reverse):
    a_cum, b_cum = lax.associative_scan(_lin_combine, (a, b), axis=1, reverse=reverse)
    return a_cum * h0[:, None, :] + b_cum


def rglru_scans(xa, conv_w, conv_b, wa, ba, wx, bx, lam, h0_fwd, h0_bwd):
    u = dwconv(xa, conv_w, conv_b, CONV_A_PAD)
    a_f, b_f = rglru_coeffs(u, wa[0], ba[0], wx[0], bx[0], lam[0])
    a_b, b_b = rglru_coeffs(u, wa[1], ba[1], wx[1], bx[1], lam[1])
    return linear_scan(a_f, b_f, h0_fwd, False), linear_scan(a_b, b_b, h0_bwd, True)


def sgu(u, v, n_chunks, norm_g, ws, bs):
    bsz, L, _ = v.shape
    vf = v.astype(jnp.float32)
    mu = jnp.mean(vf, axis=-1, keepdims=True)
    var = jnp.mean(jnp.square(vf - mu), axis=-1, keepdims=True)
    vn = (vf - mu) * lax.rsqrt(var + EPS) * norm_g.astype(jnp.float32)
    vc = vn.reshape(bsz, n_chunks, CHUNK, SGU_GROUPS, SGU_GW)
    mixed = jnp.einsum('gmn,bkngd->bkmgd', ws.astype(jnp.float32), vc)
    mixed = mixed + jnp.swapaxes(bs, 0, 1).astype(jnp.float32)[:, :, None]
    return u * mixed.reshape(bsz, L, W_SGU).astype(u.dtype)


def hyena_filters(L, w1, b1, w2, b2, w3, freq):
    f32 = jnp.float32
    t = jnp.linspace(0.0, 1.0, L, dtype=f32)[:, None]
    w = 2.0 * math.pi * jnp.arange(L, dtype=f32)[:, None] / L
    fb = jnp.linspace(1e-4, HY_BANDS - 1, HY_BANDS, dtype=f32)[None, :]
    feat = jnp.concatenate([t, jnp.cos(fb * w), -jnp.sin(fb * w)], axis=-1)
    fr = freq.astype(f32)
    h = jnp.sin(fr * (feat @ w1.astype(f32) + b1.astype(f32)))
    h = jnp.sin(fr * (h @ w2.astype(f32) + b2.astype(f32)))
    h = (h @ w3.astype(f32)).reshape(L, HY_ORDER, 2, W_HY)
    deltas = jnp.abs(jnp.linspace(HY_MIN_DECAY, HY_MAX_DECAY, W_HY, dtype=f32))
    h = h * jnp.exp(-t * deltas)[:, None, None, :]
    k = jnp.concatenate([h[:, :, 0], jnp.zeros((1, HY_ORDER, W_HY), f32), h[:0:-1, :, 1]], axis=0)
    return jnp.fft.rfft(k, axis=0)


def hyena(zc, conv_w, conv_b, kf, skip):
    L = zc.shape[1]
    zc = dwconv(zc, conv_w, conv_b, CONV_C_PAD)
    v, x1, x2 = jnp.split(zc.astype(jnp.float32), 3, axis=-1)
    skip = skip.astype(jnp.float32)

    def longconv(s, k_o, d_o):
        S = jnp.fft.rfft(s, n=2 * L, axis=1)
        return jnp.fft.irfft(S * k_o, n=2 * L, axis=1)[:, :L] + s * d_o

    s = x1 * longconv(v, kf[:, 0], skip[0])
    s = x2 * longconv(s, kf[:, 1], skip[1])
    return s.astype(zc.dtype)


def mix_merge(z, hf, hb, n_chunks, kf, lp):
    bsz, L, _ = z.shape
    dt = z.dtype
    y_a = (hf + hb).astype(dt) * jax.nn.gelu(z[..., OFF_A + W_LRU:OFF_B])
    y_b = sgu(z[..., OFF_B:OFF_B + W_SGU], z[..., OFF_B + W_SGU:OFF_C], n_chunks,
              lp['sgu_norm_g'], lp['sgu_ws'], lp['sgu_bs'])
    y_c = hyena(z[..., OFF_C:OFF_G], lp['hy_conv_w'], lp['hy_conv_b'], kf, lp['hy_skip'])
    g = jax.nn.sigmoid(z[..., OFF_G:].astype(jnp.float32)).astype(dt).reshape(bsz, L, N_BRANCH, D_MODEL)
    w_br = lp['w_br']
    merged = (g[..., 0, :] * (y_a @ w_br[0]) + g[..., 1, :] * (y_b @ w_br[1])
              + g[..., 2, :] * (y_c @ w_br[2]))
    return merged @ lp['w_out']


def route(h2, router_w, router_b):
    s = jax.nn.sigmoid((h2 @ router_w).astype(jnp.float32))
    sb = (s + router_b.astype(jnp.float32)).reshape(-1, N_GROUPS, EXP_PER_GROUP)
    gscore = lax.top_k(sb, TOP_K)[0].sum(-1)
    gsel = jnp.argmax(gscore, axis=-1)
    in_grp = jnp.take_along_axis(sb, gsel[:, None, None], axis=1)[:, 0]
    _, loc = lax.top_k(in_grp, TOP_K)
    idx = gsel[:, None] * EXP_PER_GROUP + loc
    w = jnp.take_along_axis(s, idx, axis=-1)
    w = w / jnp.sum(w, axis=-1, keepdims=True)
    return jnp.sum(jax.nn.one_hot(idx, N_EXPERTS, dtype=jnp.float32) * w[..., None], axis=1)


def moe(h, router_w, router_b, w1, w3, w2):
    bsz, L, D = h.shape
    h2 = h.reshape(-1, D)
    combine = route(h2, router_w, router_b).astype(h.dtype)
    out = jnp.zeros_like(h2)
    for e in range(N_EXPERTS):
        ye = (jax.nn.silu(h2 @ w1[e]) * (h2 @ w3[e])) @ w2[e]
        out = out + combine[:, e:e + 1] * ye
    return out.reshape(bsz, L, D)


def setup_inputs(seed: int = 0) -> dict:
    key = jax.random.key(seed)
    ks = list(jax.random.split(key, 36))
    f32 = jnp.float32

    def nrm(k, shape, scale):
        return jax.random.normal(k, shape, f32) * scale

    a0 = jax.random.uniform(ks[15], (DEPTH, 2, W_LRU), f32, 0.9, 0.999)
    root = a0 ** (1.0 / LRU_C)
    return {
        'x': nrm(ks[0], (BATCH, SEQ, D_MODEL), 1.0),
        'c': nrm(ks[1], (BATCH, D_MODEL), 1.0),
        'ctx': nrm(ks[2], (BATCH, CTX_LEN, D_MODEL), 1.0),
        'c_ctx': nrm(ks[3], (D_MODEL,), 1.0),
        'ada_w': nrm(ks[4], (DEPTH, D_MODEL, N_MOD * D_MODEL), 0.5 * D_MODEL ** -0.5),
        'ada_b': nrm(ks[5], (DEPTH, N_MOD * D_MODEL), 0.01),
        'norm1_g': 1.0 + nrm(ks[6], (DEPTH, D_MODEL), 0.02),
        'norm2_g': 1.0 + nrm(ks[7], (DEPTH, D_MODEL), 0.02),
        'w_in': nrm(ks[8], (DEPTH, D_MODEL, D_IN), D_MODEL ** -0.5),
        'conv_a_w': nrm(ks[9], (DEPTH, CONV_A, W_LRU), CONV_A ** -0.5),
        'conv_a_b': nrm(ks[10], (DEPTH, W_LRU), 0.01),
        'lru_wa': nrm(ks[11], (DEPTH, 2, LRU_HEADS, LRU_BW, LRU_BW), LRU_BW ** -0.5),
        'lru_ba': nrm(ks[12], (DEPTH, 2, W_LRU), 0.01),
        'lru_wx': nrm(ks[13], (DEPTH, 2, LRU_HEADS, LRU_BW, LRU_BW), LRU_BW ** -0.5),
        'lru_bx': nrm(ks[14], (DEPTH, 2, W_LRU), 0.01),
        'lru_lam': jnp.log(root) - jnp.log1p(-root),
        'sgu_norm_g': 1.0 + nrm(ks[16], (DEPTH, W_SGU), 0.02),
        'sgu_ws': nrm(ks[17], (DEPTH, SGU_GROUPS, CHUNK, CHUNK), CHUNK ** -0.5),
        'sgu_bs': 1.0 + nrm(ks[18], (DEPTH, SGU_GROUPS, CHUNK), 0.01),
        'hy_conv_w': nrm(ks[19], (DEPTH, CONV_C, 3 * W_HY), CONV_C ** -0.5),
        'hy_conv_b': nrm(ks[20], (DEPTH, 3 * W_HY), 0.01),
        'hy_w1': nrm(ks[21], (DEPTH, HY_EMB, HY_HID), HY_EMB ** -0.5),
        'hy_b1': nrm(ks[22], (DEPTH, HY_HID), 0.1),
        'hy_w2': nrm(ks[23], (DEPTH, HY_HID, HY_HID), HY_HID ** -0.5),
        'hy_b2': nrm(ks[24], (DEPTH, HY_HID), 0.1),
        'hy_w3': nrm(ks[25], (DEPTH, HY_HID, HY_ORDER * 2 * W_HY), HY_FILTER_SCALE * HY_HID ** -0.5),
        'hy_freq': 1.0 + nrm(ks[26], (DEPTH, HY_HID), 0.02),
        'hy_skip': nrm(ks[27], (DEPTH, HY_ORDER, W_HY), 0.1),
        'w_br': nrm(ks[28], (DEPTH, N_BRANCH, W_BRANCH, D_MODEL), W_BRANCH ** -0.5),
        'w_out': nrm(ks[29], (DEPTH, D_MODEL, D_MODEL), D_MODEL ** -0.5),
        'exp_w1': nrm(ks[30], (DEPTH, N_EXPERTS, D_MODEL, D_EXPERT), D_MODEL ** -0.5),
        'exp_w3': nrm(ks[31], (DEPTH, N_EXPERTS, D_MODEL, D_EXPERT), D_MODEL ** -0.5),
        'exp_w2': nrm(ks[32], (DEPTH, N_EXPERTS, D_EXPERT, D_MODEL), D_EXPERT ** -0.5),
        'router_w': nrm(ks[33], (D_MODEL, N_EXPERTS), D_MODEL ** -0.5),
        'router_b': nrm(ks[34], (N_EXPERTS,), 0.01),
        'final_g': 1.0 + nrm(ks[35], (D_MODEL,), 0.02),
    }


def reference(x, c, ctx, c_ctx, ada_w, ada_b, norm1_g, norm2_g, w_in, conv_a_w, conv_a_b,
              lru_wa, lru_ba, lru_wx, lru_bx, lru_lam, sgu_norm_g, sgu_ws, sgu_bs,
              hy_conv_w, hy_conv_b, hy_w1, hy_b1, hy_w2, hy_b2, hy_w3, hy_freq, hy_skip,
              w_br, w_out, exp_w1, exp_w3, exp_w2, router_w, router_b, final_g):
    bsz, n_lat, _ = x.shape
    n_ctx = ctx.shape[1]
    rows = n_lat // GRID_W
    n_chunks_lat = rows // ROWS_PER_CHUNK
    n_chunks_ctx = n_ctx // CHUNK
    xl, xc = x, ctx
    s_lat = jax.nn.silu(c)
    s_ctx = jax.nn.silu(c_ctx)
    h_zero = jnp.zeros((bsz, W_LRU), jnp.float32)
    for l in range(DEPTH):
        last = l == DEPTH - 1
        lp = {'sgu_norm_g': sgu_norm_g[l], 'sgu_ws': sgu_ws[l], 'sgu_bs': sgu_bs[l],
              'hy_conv_w': hy_conv_w[l], 'hy_conv_b': hy_conv_b[l], 'hy_skip': hy_skip[l],
              'w_br': w_br[l], 'w_out': w_out[l]}
        mod_l = (s_lat @ ada_w[l] + ada_b[l])[:, None, :]
        mod_c = s_ctx @ ada_w[l] + ada_b[l]
        sh1, sc1, g1, sh2, sc2, g2 = jnp.split(mod_l, N_MOD, axis=-1)
        csh1, csc1, cg1, csh2, csc2, cg2 = jnp.split(mod_c, N_MOD, axis=-1)

        hl = modulate(xl, norm1_g[l], sh1, sc1)
        hc = modulate(xc, norm1_g[l], csh1, csc1)
        zl = hl @ w_in[l]
        zc = hc @ (w_in[l][:, OFF_A:OFF_A + W_LRU] if last else w_in[l])
        cf, cb = rglru_scans(zc[..., OFF_A:OFF_A + W_LRU], conv_a_w[l], conv_a_b[l], lru_wa[l], lru_ba[l],
                             lru_wx[l], lru_bx[l], lru_lam[l], h_zero, h_zero)
        lf, lb = rglru_scans(zl[..., OFF_A:OFF_A + W_LRU], conv_a_w[l], conv_a_b[l], lru_wa[l], lru_ba[l],
                             lru_wx[l], lru_bx[l], lru_lam[l], cf[:, -1], cb[:, 0])
        kf_lat = hyena_filters(n_lat, hy_w1[l], hy_b1[l], hy_w2[l], hy_b2[l], hy_w3[l], hy_freq[l])
        xl = xl + g1 * mix_merge(zl, lf, lb, n_chunks_lat, kf_lat, lp)

        xl = xl + g2 * moe(modulate(xl, norm2_g[l], sh2, sc2), router_w, router_b,
                           exp_w1[l], exp_w3[l], exp_w2[l])

        if not last:
            kf_ctx = hyena_filters(n_ctx, hy_w1[l], hy_b1[l], hy_w2[l], hy_b2[l], hy_w3[l], hy_freq[l])
            xc = xc + cg1 * mix_merge(zc, cf, cb, n_chunks_ctx, kf_ctx, lp)
            xc = xc + cg2 * moe(modulate(xc, norm2_g[l], csh2, csc2), router_w, router_b,
                                exp_w1[l], exp_w3[l], exp_w2[l])
    return rmsnorm(xl, final_g)
```

```python
import functools
import math

import jax
import jax.numpy as jnp
import numpy as np
from jax import lax
from jax.experimental import pallas as pl
from jax.experimental.pallas import tpu as pltpu

F32 = jnp.float32
BF16 = jnp.bfloat16
I32 = jnp.int32

EPS = 1e-6
LRU_C = 8.0
SGU_CHUNK = 128
HY_TARGET = 1e-2
HY_FAST_PCT = 0.3
HY_SLOW_PCT = 1.5
HY_MAX_DECAY = math.log(HY_TARGET) / HY_FAST_PCT
HY_MIN_DECAY = math.log(HY_TARGET) / HY_SLOW_PCT
N_GROUPS = 4
EXP_PER_GROUP = 4
N_PAIRS = 6
N_CLASSES = N_GROUPS * N_PAIRS
MOD_ROWS = 16
V7X_VMEM_LIMIT = 56 * 1024 * 1024
HALO = 16


def _params(sem, vmem=V7X_VMEM_LIMIT):
    return pltpu.CompilerParams(dimension_semantics=sem, vmem_limit_bytes=vmem)


def _sds(shape, dtype):
    return jax.ShapeDtypeStruct(shape, dtype)


def _tile(n, pref):
    t = min(pref, n)
    while n % t:
        t //= 2
    return t


def _modulate(x, g, shift, scale):
    ms = jnp.mean(x * x, axis=-1, keepdims=True)
    y = x * lax.rsqrt(ms + EPS) * g
    return y * (1.0 + scale) + shift


def _mod_kernel(c_ref, w_ref, b_ref, o_ref):
    c = c_ref[...]
    s = (c * jax.nn.sigmoid(c)).astype(BF16)
    o_ref[...] = jnp.dot(s, w_ref[...].astype(BF16), preferred_element_type=F32) + b_ref[...]


def _mod_all(c_all, ada_w, ada_b):
    depth, d, n = ada_w.shape
    tn = _tile(n, 1024)
    return pl.pallas_call(
        _mod_kernel,
        grid=(depth, n // tn),
        in_specs=[pl.BlockSpec((MOD_ROWS, d), lambda l, j: (0, 0)),
                  pl.BlockSpec((None, d, tn), lambda l, j: (l, 0, j)),
                  pl.BlockSpec((None, 1, tn), lambda l, j: (l, 0, j))],
        out_specs=pl.BlockSpec((None, MOD_ROWS, tn), lambda l, j: (l, 0, j)),
        out_shape=_sds((depth, MOD_ROWS, n), F32),
        compiler_params=_params(("parallel", "parallel")),
        name="adaln_mod",
    )(c_all, ada_w, ada_b.reshape(depth, 1, n))


def _win_kernel(fuse_moe, *refs):
    if fuse_moe:
        x_ref, moe_ref, g2_ref, sh_ref, sc_ref, g_ref, w_ref, o_ref, xo_ref, h_ref = refs
    else:
        x_ref, sh_ref, sc_ref, g_ref, w_ref, o_ref, h_ref = refs

    @pl.when(pl.program_id(1) == 0)
    def _():
        x = x_ref[...]
        if fuse_moe:
            x = x + g2_ref[...] * moe_ref[...]
            xo_ref[...] = x
        h_ref[...] = _modulate(x, g_ref[...], sh_ref[...], sc_ref[...]).astype(BF16)

    o_ref[...] = jnp.dot(h_ref[...], w_ref[...], preferred_element_type=F32).astype(o_ref.dtype)


def _win_call(x, moe, mod3, mod3_prev, norm_g, w_bf, rows, seq_len, n_batch):
    d = x.shape[1]
    n = w_bf.shape[1]
    tm = _tile(math.gcd(seq_len, rows), 512)
    tn = d // 2
    fuse = moe is not None

    def mrow(i):
        return jnp.minimum((i * tm) // seq_len, n_batch)

    x_spec = pl.BlockSpec((tm, d), lambda i, j: (i, 0))

    def mod_spec(col):
        return pl.BlockSpec((None, 1, d), lambda i, j: (mrow(i), 0, col))

    in_specs = [x_spec]
    args = [x]
    if fuse:
        in_specs += [x_spec, mod_spec(5)]
        args += [moe, mod3_prev]
    in_specs += [mod_spec(0), mod_spec(1), pl.BlockSpec((1, d), lambda i, j: (0, 0)),
                 pl.BlockSpec((d, tn), lambda i, j: (0, j))]
    args += [mod3, mod3, norm_g.reshape(1, d), w_bf]
    out_specs = [pl.BlockSpec((tm, tn), lambda i, j: (i, j))]
    out_shape = [_sds((x.shape[0], n), BF16)]
    if fuse:
        out_specs.append(x_spec)
        out_shape.append(_sds(x.shape, F32))
    res = pl.pallas_call(
        functools.partial(_win_kernel, fuse),
        grid=(rows // tm, n // tn),
        in_specs=in_specs, out_specs=out_specs, out_shape=out_shape,
        scratch_shapes=[pltpu.VMEM((tm, d), BF16)],
        compiler_params=_params(("parallel", "arbitrary")),
        name="in_proj",
    )(*args)
    return (res[0], res[1]) if fuse else (res[0], x)


def _lru_tile(d, j, nt_c, nt_l):
    is_ctx = j < nt_c
    nt = jnp.where(is_ctx, nt_c, nt_l)
    jl = jnp.where(is_ctx, j, j - nt_c)
    return is_ctx, jl + d * (nt - 1 - 2 * jl), nt


def _lru_kernel(tl, nt_c, nt_l, heads, xm_ref, xp_ref, xn_ref, cw_ref, cb_ref, wg_ref, bg_ref, lam_ref,
                o_ref, pad_ref, a_ref, b_ref, hs_ref, hc_ref):
    d = pl.program_id(0)
    j = pl.program_id(2)
    _, jt, nt = _lru_tile(d, j, nt_c, nt_l)
    w = a_ref.shape[1]
    bw = w // heads

    pad_ref[8:8 + tl, :] = xm_ref[...].astype(F32)
    prev = xp_ref[...].astype(F32)[HALO - 8:HALO]
    nxt = xn_ref[...].astype(F32)[0:8]
    pad_ref[0:8, :] = jnp.where(jt > 0, prev, 0.0)
    pad_ref[8 + tl:16 + tl, :] = jnp.where(jt < nt - 1, nxt, 0.0)
    cw = cw_ref[...]
    u = (cw[0:1] * pad_ref[7:7 + tl, :] + cw[1:2] * pad_ref[8:8 + tl, :]
         + cw[2:3] * pad_ref[9:9 + tl, :] + cw[3:4] * pad_ref[10:10 + tl, :]) + cb_ref[...]
    ub = u.astype(BF16)
    lam = lam_ref[...]
    sp = jnp.maximum(-lam, 0.0) + jnp.log1p(jnp.exp(-jnp.abs(lam)))
    for hd in range(heads):
        cs = slice(hd * bw, (hd + 1) * bw)
        g = jnp.dot(ub[:, cs], wg_ref[hd], preferred_element_type=F32) + bg_ref[hd]
        r = jax.nn.sigmoid(g[:, :bw])
        i = jax.nn.sigmoid(g[:, bw:])
        log_a = (-LRU_C) * r * sp[:, cs]
        a = jnp.exp(log_a)
        a_ref[:, cs] = a
        b_ref[:, cs] = jnp.sqrt(1.0 - a * a) * (i * u[:, cs])

    @pl.when(j == 0)
    def _():
        hc_ref[...] = jnp.zeros_like(hc_ref)

    def step(s, h):
        t = s + d * (tl - 1 - 2 * s)
        h = a_ref[pl.ds(t, 1), :] * h + b_ref[pl.ds(t, 1), :]
        hs_ref[pl.ds(t, 1), :] = h
        return h

    h = lax.fori_loop(0, tl, step, hc_ref[...], unroll=8)
    hc_ref[...] = h
    o_ref[...] = hs_ref[...].astype(o_ref.dtype)


def _lru_call(z, lw, n_lat, n_ctx, n_batch):
    w = lw["conv_w"].shape[1]
    heads = lw["wg"].shape[1]
    tl = _tile(math.gcd(n_lat, n_ctx), 512)
    nt_c, nt_l = n_ctx // tl, n_lat // tl
    ctx_blk0 = n_batch * nt_l
    per = tl // HALO
    last_h = n_batch * (n_lat + n_ctx) // HALO - 1

    def blk(d, b, j):
        is_ctx, jt, _ = _lru_tile(d, j, nt_c, nt_l)
        return jnp.where(is_ctx, ctx_blk0 + b * nt_c + jt, b * nt_l + jt)

    in_specs = [pl.BlockSpec((tl, w), lambda d, b, j: (blk(d, b, j), 0)),
                pl.BlockSpec((HALO, w), lambda d, b, j: (jnp.maximum(blk(d, b, j) * per - 1, 0), 0)),
                pl.BlockSpec((HALO, w), lambda d, b, j: (jnp.minimum((blk(d, b, j) + 1) * per, last_h), 0)),
                pl.BlockSpec((4, w), lambda d, b, j: (0, 0)),
                pl.BlockSpec((1, w), lambda d, b, j: (0, 0)),
                pl.BlockSpec((None, heads, w // heads, 2 * (w // heads)), lambda d, b, j: (d, 0, 0, 0)),
                pl.BlockSpec((None, heads, 1, 2 * (w // heads)), lambda d, b, j: (d, 0, 0, 0)),
                pl.BlockSpec((None, 1, w), lambda d, b, j: (d, 0, 0))]
    return pl.pallas_call(
        functools.partial(_lru_kernel, tl, nt_c, nt_l, heads),
        grid=(2, n_batch, nt_c + nt_l),
        in_specs=in_specs,
        out_specs=pl.BlockSpec((None, tl, w), lambda d, b, j: (d, blk(d, b, j), 0)),
        out_shape=_sds((2, n_batch * (n_lat + n_ctx), w), BF16),
        scratch_shapes=[pltpu.VMEM((tl + 16, w), F32), pltpu.VMEM((tl, w), F32),
                        pltpu.VMEM((tl, w), F32), pltpu.VMEM((tl, w), F32), pltpu.VMEM((1, w), F32)],
        compiler_params=_params(("arbitrary", "arbitrary", "arbitrary")),
        name="rglru_scan",
    )(z, z, z, lw["conv_w"], lw["conv_b"], lw["wg"], lw["bg"], lw["lam"])


def _sgu_kernel(groups, u_ref, v_ref, g_ref, ws_ref, bias_ref, o_ref):
    v = v_ref[...].astype(F32)
    mu = jnp.mean(v, axis=-1, keepdims=True)
    vc = v - mu
    var = jnp.mean(vc * vc, axis=-1, keepdims=True)
    vn = (vc * lax.rsqrt(var + EPS) * g_ref[...]).astype(BF16)
    ts, w = v.shape
    gw = w // groups
    for ck in range(ts // SGU_CHUNK):
        rs = slice(ck * SGU_CHUNK, (ck + 1) * SGU_CHUNK)
        for g in range(groups):
            cs = slice(g * gw, (g + 1) * gw)
            mixed = jnp.dot(ws_ref[g], vn[rs, cs], preferred_element_type=F32) + bias_ref[:, cs]
            o_ref[rs, cs] = (u_ref[rs, cs].astype(F32) * mixed).astype(o_ref.dtype)


def _sgu_call(z, norm_g, ws_bf, bias_full, rows):
    groups = ws_bf.shape[0]
    w = norm_g.shape[0]
    ts = _tile(rows, 512)
    return pl.pallas_call(
        functools.partial(_sgu_kernel, groups),
        grid=(rows // ts,),
        in_specs=[pl.BlockSpec((ts, w), lambda i: (i, 2)),
                  pl.BlockSpec((ts, w), lambda i: (i, 3)),
                  pl.BlockSpec((1, w), lambda i: (0, 0)),
                  pl.BlockSpec((groups, SGU_CHUNK, SGU_CHUNK), lambda i: (0, 0, 0)),
                  pl.BlockSpec((SGU_CHUNK, w), lambda i: (0, 0))],
        out_specs=pl.BlockSpec((ts, w), lambda i: (i, 0)),
        out_shape=_sds((rows, w), BF16),
        compiler_params=_params(("parallel",)),
        name="sgu",
    )(z, z, norm_g.reshape(1, w), ws_bf, bias_full)


def _hconv_kernel(tl, nt, xm_ref, xp_ref, xn_ref, cw_ref, cb_ref, o_ref, pad_ref):
    j = pl.program_id(2)
    pad_ref[8:8 + tl, :] = xm_ref[...].astype(F32)
    prev = xp_ref[...].astype(F32)[HALO - 8:HALO]
    nxt = xn_ref[...].astype(F32)[0:8]
    pad_ref[0:8, :] = jnp.where(j > 0, prev, 0.0)
    pad_ref[8 + tl:16 + tl, :] = jnp.where(j < nt - 1, nxt, 0.0)
    cw = cw_ref[...]
    u = (cw[0:1] * pad_ref[7:7 + tl, :] + cw[1:2] * pad_ref[8:8 + tl, :]
         + cw[2:3] * pad_ref[9:9 + tl, :]) + cb_ref[...]
    o_ref[...] = u.astype(o_ref.dtype)


def _hconv_call(z, conv_w, conv_b, row0, seq_len, n_batch):
    w = conv_w.shape[1] // 3
    tl = _tile(seq_len, 512)
    nt = seq_len // tl
    blk0 = row0 // tl
    hb0 = row0 // HALO
    per = tl // HALO
    last_h = (row0 + n_batch * seq_len) // HALO - 1
    return pl.pallas_call(
        functools.partial(_hconv_kernel, tl, nt),
        grid=(3, n_batch, nt),
        in_specs=[pl.BlockSpec((tl, w), lambda s, b, j: (blk0 + b * nt + j, 4 + s)),
                  pl.BlockSpec((HALO, w), lambda s, b, j: (jnp.maximum(hb0 + (b * nt + j) * per - 1, 0), 4 + s)),
                  pl.BlockSpec((HALO, w), lambda s, b, j: (jnp.minimum(hb0 + (b * nt + j + 1) * per, last_h), 4 + s)),
                  pl.BlockSpec((3, w), lambda s, b, j: (0, s)),
                  pl.BlockSpec((1, w), lambda s, b, j: (0, s))],
        out_specs=pl.BlockSpec((None, tl, w), lambda s, b, j: (s, b * nt + j, 0)),
        out_shape=_sds((3, n_batch * seq_len, w), BF16),
        scratch_shapes=[pltpu.VMEM((tl + 16, w), F32)],
        compiler_params=_params(("parallel", "parallel", "arbitrary")),
        name="hyena_conv",
    )(z, z, z, conv_w, conv_b.reshape(1, 3 * w))


def _hfilt_kernel(feat_ref, t_ref, dl_ref, w1_ref, b1_ref, w2_ref, b2_ref, w3_ref, fr_ref, o_ref):
    hi = lax.Precision.HIGHEST
    fr = fr_ref[...]
    h = jnp.sin(fr * (jnp.dot(feat_ref[...], w1_ref[...], precision=hi, preferred_element_type=F32) + b1_ref[...]))
    h = jnp.sin(fr * (jnp.dot(h, w2_ref[...], precision=hi, preferred_element_type=F32) + b2_ref[...]))
    h = jnp.dot(h, w3_ref[...], precision=hi, preferred_element_type=F32)
    w = h.shape[1] // 2
    h = h * jnp.exp(-t_ref[...] * dl_ref[...])
    fwd = h[:, :w]
    row = lax.broadcasted_iota(I32, (h.shape[0], 1), 0) + pl.program_id(1) * h.shape[0]
    bwd = jnp.where(row > 0, h[:, w:], 0.0)
    o_ref[:, :w] = (fwd + bwd).astype(o_ref.dtype)
    o_ref[:, w:] = (fwd - bwd).astype(o_ref.dtype)


def _hfilt_call(consts, hw):
    feat, tcol, deltas = consts["feat"], consts["t"], consts["deltas"]
    seq_len, fpad = feat.shape
    hid = hw["w2"].shape[0]
    w = deltas.shape[1]
    orders = hw["w3"].shape[1] // (2 * w)
    tr = _tile(seq_len, 512)
    return pl.pallas_call(
        _hfilt_kernel,
        grid=(orders, seq_len // tr),
        in_specs=[pl.BlockSpec((tr, fpad), lambda o, i: (i, 0)),
                  pl.BlockSpec((tr, 1), lambda o, i: (i, 0)),
                  pl.BlockSpec((1, 2 * w), lambda o, i: (0, 0)),
                  pl.BlockSpec((fpad, hid), lambda o, i: (0, 0)),
                  pl.BlockSpec((1, hid), lambda o, i: (0, 0)),
                  pl.BlockSpec((hid, hid), lambda o, i: (0, 0)),
                  pl.BlockSpec((1, hid), lambda o, i: (0, 0)),
                  pl.BlockSpec((hid, 2 * w), lambda o, i: (0, o)),
                  pl.BlockSpec((1, hid), lambda o, i: (0, 0))],
        out_specs=pl.BlockSpec((None, tr, 2 * w), lambda o, i: (o, i, 0)),
        out_shape=_sds((orders, seq_len, 2 * w), BF16),
        compiler_params=_params(("parallel", "parallel")),
        name="hyena_filter",
    )(feat, tcol, jnp.concatenate([deltas, deltas], axis=1), hw["w1"], hw["b1"], hw["w2"], hw["b2"],
      hw["w3"], hw["freq"])


def _hspec_kernel(tk, g_ref, p_ref, q_ref, ma_ref, mb_ref):
    sp = jnp.dot(g_ref[...], p_ref[...], preferred_element_type=F32)
    sq = jnp.dot(g_ref[...], q_ref[...], preferred_element_type=F32)
    re = sp[:tk]
    nyq = sp[tk:]
    kfi = sq[tk:]
    first = (lax.broadcasted_iota(I32, (tk, 1), 0) == 0) & (pl.program_id(2) == 0)
    ma_ref[:tk, :] = re
    ma_ref[tk:, :] = jnp.where(first, nyq, re)
    kfi = jnp.where(first, 0.0, kfi)
    mb_ref[:tk, :] = -kfi
    mb_ref[tk:, :] = kfi


def _hspec_call(gf, pq, tk):
    orders, seq_len, w2 = pq.shape
    w = w2 // 2
    n2 = gf.shape[0]
    tn = _tile(w, 512)
    nn = w // tn
    return pl.pallas_call(
        functools.partial(_hspec_kernel, tk),
        grid=(orders, nn, n2 // (2 * tk)),
        in_specs=[pl.BlockSpec((2 * tk, seq_len), lambda o, n, m: (m, 0)),
                  pl.BlockSpec((None, seq_len, tn), lambda o, n, m: (o, 0, n)),
                  pl.BlockSpec((None, seq_len, tn), lambda o, n, m: (o, 0, nn + n))],
        out_specs=[pl.BlockSpec((None, 2 * tk, tn), lambda o, n, m: (o, m, n)),
                   pl.BlockSpec((None, 2 * tk, tn), lambda o, n, m: (o, m, n))],
        out_shape=[_sds((orders, n2, w), F32), _sds((orders, n2, w), F32)],
        compiler_params=_params(("parallel", "parallel", "arbitrary")),
        name="hyena_filter_spectrum",
    )(gf, pq, pq)


def _hfwd_kernel(tk, g_ref, x_ref, ma_ref, mb_ref, o_ref):
    s = jnp.dot(g_ref[...], x_ref[...], preferred_element_type=F32)
    sw = jnp.concatenate([s[tk:], s[:tk]], axis=0)
    o_ref[...] = (s * ma_ref[...] + sw * mb_ref[...]).astype(o_ref.dtype)


def _hfwd_call(gf, x2d, xblk0, ma, mb, order, tk, n_batch):
    n2, seq_len = gf.shape
    w = x2d.shape[1]
    return pl.pallas_call(
        functools.partial(_hfwd_kernel, tk),
        grid=(n2 // (2 * tk), n_batch),
        in_specs=[pl.BlockSpec((2 * tk, seq_len), lambda m, b: (m, 0)),
                  pl.BlockSpec((seq_len, w), lambda m, b: (xblk0 + b, 0)),
                  pl.BlockSpec((None, 2 * tk, w), lambda m, b: (order, m, 0)),
                  pl.BlockSpec((None, 2 * tk, w), lambda m, b: (order, m, 0))],
        out_specs=pl.BlockSpec((None, 2 * tk, w), lambda m, b: (b, m, 0)),
        out_shape=_sds((n_batch, n2, w), BF16),
        compiler_params=_params(("parallel", "arbitrary")),
        name="hyena_dft",
    )(gf, x2d, ma, mb)


def _hinv_kernel(g_ref, p_ref, v_ref, x_ref, sk_ref, o_ref):
    y = jnp.dot(g_ref[...], p_ref[...], preferred_element_type=F32)
    v = v_ref[...].astype(F32)
    o_ref[...] = (x_ref[...].astype(F32) * (y + v * sk_ref[...])).astype(o_ref.dtype)


def _hinv_call(gi, p, v2d, vblk0, x2d, xblk0, skip3, order, n_batch):
    seq_len, n2 = gi.shape
    w = v2d.shape[1]
    tm = _tile(seq_len, 512)
    tn = _tile(w, 512)
    nm = seq_len // tm
    return pl.pallas_call(
        _hinv_kernel,
        grid=(nm, n_batch, w // tn),
        in_specs=[pl.BlockSpec((tm, n2), lambda m, b, n: (m, 0)),
                  pl.BlockSpec((None, n2, tn), lambda m, b, n: (b, 0, n)),
                  pl.BlockSpec((tm, tn), lambda m, b, n: (vblk0 + b * nm + m, n)),
                  pl.BlockSpec((tm, tn), lambda m, b, n: (xblk0 + b * nm + m, n)),
                  pl.BlockSpec((None, 1, tn), lambda m, b, n: (order, 0, n))],
        out_specs=pl.BlockSpec((tm, tn), lambda m, b, n: (b * nm + m, n)),
        out_shape=_sds((n_batch * seq_len, w), BF16),
        compiler_params=_params(("parallel", "arbitrary", "arbitrary")),
        name="hyena_idft",
    )(gi, p, v2d, x2d, skip3)


def _dft_consts(seq_len, w, emb, tk):
    f32 = F32
    k = jnp.arange(seq_len, dtype=I32)
    m = (k[:, None] * k[None, :]) % (2 * seq_len)
    ang = m.astype(f32) * (math.pi / seq_len)
    c = jnp.cos(ang)
    s = jnp.sin(ang)
    nyq = jnp.where(k % 2 == 0, 1.0, -1.0).astype(f32)
    nb = seq_len // tk
    im_f = (-s).at[0].set(nyq)
    gf = jnp.stack([c.reshape(nb, tk, seq_len), im_f.reshape(nb, tk, seq_len)], axis=1)
    gf = gf.reshape(2 * seq_len, seq_len).astype(BF16)
    ci = (c / seq_len).at[:, 0].multiply(0.5)
    si = (-s / seq_len).at[:, 0].set(nyq / (2 * seq_len))
    gi = jnp.stack([ci.reshape(seq_len, nb, tk), si.reshape(seq_len, nb, tk)], axis=2)
    gi = gi.reshape(seq_len, 2 * seq_len).astype(BF16)
    bands = (emb - 1) // 2
    t = jnp.linspace(0.0, 1.0, seq_len, dtype=f32)[:, None]
    wv = 2.0 * math.pi * jnp.arange(seq_len, dtype=f32)[:, None] / seq_len
    fb = jnp.linspace(1e-4, bands - 1, bands, dtype=f32)[None, :]
    feat = jnp.concatenate([t, jnp.cos(fb * wv), -jnp.sin(fb * wv)], axis=-1)
    fpad = 128
    feat = jnp.pad(feat, ((0, 0), (0, fpad - emb)))
    deltas = jnp.abs(jnp.linspace(HY_MIN_DECAY, HY_MAX_DECAY, w, dtype=f32))[None, :]
    return {"gf": gf, "gi": gi, "feat": feat, "t": t, "deltas": deltas, "tk": tk}


def _hyena_call(z, consts, hw, row0, seq_len, n_batch):
    tk = consts["tk"]
    w = hw["skip3"].shape[2]
    hc = _hconv_call(z, hw["conv_w"], hw["conv_b"], row0, seq_len, n_batch)
    hc2 = hc.reshape(3 * n_batch * seq_len, w)
    pq = _hfilt_call(consts, hw)
    ma, mb = _hspec_call(consts["gf"], pq, tk)
    tm = _tile(seq_len, 512)
    sec = n_batch * seq_len // tm
    p0 = _hfwd_call(consts["gf"], hc2, 0, ma, mb, 0, tk, n_batch)
    s1 = _hinv_call(consts["gi"], p0, hc2, 0, hc2, sec, hw["skip3"], 0, n_batch)
    p1 = _hfwd_call(consts["gf"], s1, 0, ma, mb, 1, tk, n_batch)
    return _hinv_call(consts["gi"], p1, s1, 0, hc2, 2 * sec, hw["skip3"], 1, n_batch)


def _merge_kernel(lat_tiles, hf_ref, hb_ref, ga_ref, yb_ref, ycl_ref, ycc_ref, g0a, g0b, g1a, g1b, g2a, g2b,
                  x_ref, gate_ref, wbr_ref, wout_ref, o_ref, m_ref):
    ga = ga_ref[...].astype(F32)
    ya = ((hf_ref[...].astype(F32) + hb_ref[...].astype(F32)) * jax.nn.gelu(ga)).astype(BF16)
    yc = jnp.where(pl.program_id(0) < lat_tiles, ycl_ref[...], ycc_ref[...])
    ys = (ya, yb_ref[...], yc)
    w = ya.shape[1]
    gates = ((g0a, g0b), (g1a, g1b), (g2a, g2b))
    for half in range(2):
        cs = slice(half * w, (half + 1) * w)
        acc = None
        for br in range(3):
            gt = jax.nn.sigmoid(gates[br][half][...].astype(F32))
            t = gt * jnp.dot(ys[br], wbr_ref[br, :, cs], preferred_element_type=F32)
            acc = t if acc is None else acc + t
        m_ref[:, cs] = acc.astype(BF16)
    out = jnp.dot(m_ref[...], wout_ref[...], preferred_element_type=F32)
    o_ref[...] = x_ref[...] + gate_ref[...] * out


def _merge_call(hlru, z, yb, yc_lat, yc_ctx, x, mod3, wbr_bf, wout_bf, rows, seq_len, n_batch):
    d = x.shape[1]
    w = d // 2
    tm = _tile(math.gcd(seq_len, rows), 256)
    lat_tiles = yc_lat.shape[0] // tm
    ctx_last = yc_ctx.shape[0] // tm - 1

    def mrow(i):
        return jnp.minimum((i * tm) // seq_len, n_batch)

    def zcol(c):
        return pl.BlockSpec((tm, w), lambda i: (i, c))

    const = dict(pipeline_mode=pl.Buffered(1))
    in_specs = [pl.BlockSpec((None, tm, w), lambda i: (0, i, 0)),
                pl.BlockSpec((None, tm, w), lambda i: (1, i, 0)),
                zcol(1), zcol(0),
                pl.BlockSpec((tm, w), lambda i: (jnp.minimum(i, lat_tiles - 1), 0)),
                pl.BlockSpec((tm, w), lambda i: (jnp.clip(i - lat_tiles, 0, ctx_last), 0))] + [
                zcol(7 + c) for c in range(6)] + [
                pl.BlockSpec((tm, d), lambda i: (i, 0)),
                pl.BlockSpec((None, 1, d), lambda i: (mrow(i), 0, 2)),
                pl.BlockSpec((3, w, d), lambda i: (0, 0, 0), **const),
                pl.BlockSpec((d, d), lambda i: (0, 0), **const)]
    return pl.pallas_call(
        functools.partial(_merge_kernel, lat_tiles),
        grid=(rows // tm,),
        in_specs=in_specs,
        out_specs=pl.BlockSpec((tm, d), lambda i: (i, 0)),
        out_shape=_sds((rows, d), F32),
        scratch_shapes=[pltpu.VMEM((tm, d), BF16)],
        compiler_params=_params(("parallel",)),
        name="merge_out_proj",
    )(hlru, hlru, z, yb, yc_lat, yc_ctx, z, z, z, z, z, z, x, mod3, wbr_bf, wout_bf)


def _router_kernel(x_ref, sh_ref, sc_ref, g_ref, rw_ref, rb_ref, h_ref, cls_ref, wt_ref):
    h = _modulate(x_ref[...], g_ref[...], sh_ref[...], sc_ref[...])
    h_ref[...] = h
    logits = lax.dot_general(rw_ref[...], h, (((1,), (1,)), ((), ())),
                             precision=lax.Precision.HIGHEST, preferred_element_type=F32)
    s = jax.nn.sigmoid(logits)
    sb = s + rb_ref[...]
    rows_b = [sb[e:e + 1, :] for e in range(N_GROUPS * EXP_PER_GROUP)]
    rows_s = [s[e:e + 1, :] for e in range(N_GROUPS * EXP_PER_GROUP)]
    gscore = []
    for g in range(N_GROUPS):
        v = rows_b[EXP_PER_GROUP * g:EXP_PER_GROUP * (g + 1)]
        best = None
        for a in range(EXP_PER_GROUP):
            for b in range(a + 1, EXP_PER_GROUP):
                pr = v[a] + v[b]
                best = pr if best is None else jnp.maximum(best, pr)
        gscore.append(best)
    best = gscore[0]
    gsel = jnp.zeros(best.shape, I32)
    for g in range(1, N_GROUPS):
        better = gscore[g] > best
        best = jnp.where(better, gscore[g], best)
        gsel = jnp.where(better, g, gsel)

    def pick(rows, i):
        out = rows[i]
        for g in range(1, N_GROUPS):
            out = jnp.where(gsel == g, rows[EXP_PER_GROUP * g + i], out)
        return out

    vb = [pick(rows_b, i) for i in range(EXP_PER_GROUP)]
    vs = [pick(rows_s, i) for i in range(EXP_PER_GROUP)]
    sel = []
    for i in range(EXP_PER_GROUP):
        rank = jnp.zeros(best.shape, I32)
        for jx in range(EXP_PER_GROUP):
            if jx == i:
                continue
            ahead = (vb[jx] > vb[i]) if jx > i else (vb[jx] >= vb[i])
            rank = rank + ahead.astype(I32)
        sel.append(rank < 2)
    j_hi = jnp.where(sel[3], 3, jnp.where(sel[2], 2, 1))
    w_lo = jnp.where(sel[0], vs[0], jnp.where(sel[1], vs[1], vs[2]))
    w_hi = jnp.where(sel[3], vs[3], jnp.where(sel[2], vs[2], vs[1]))
    den = w_lo + w_hi
    pair = jnp.where(sel[0], j_hi - 1, jnp.where(sel[1], j_hi + 1, 5))
    cls_ref[...] = gsel * N_PAIRS + pair
    wt_ref[0:1, :] = w_lo / den
    wt_ref[1:2, :] = w_hi / den


def _router_call(x, mod3, norm_g, rw_t, rb, rows, seq_len, n_batch):
    d = x.shape[1]
    e = rw_t.shape[0]
    tm = _tile(math.gcd(seq_len, rows), 512)

    def mrow(i):
        return jnp.minimum((i * tm) // seq_len, n_batch)

    return pl.pallas_call(
        _router_kernel,
        grid=(rows // tm,),
        in_specs=[pl.BlockSpec((tm, d), lambda i: (i, 0)),
                  pl.BlockSpec((None, 1, d), lambda i: (mrow(i), 0, 3)),
                  pl.BlockSpec((None, 1, d), lambda i: (mrow(i), 0, 4)),
                  pl.BlockSpec((1, d), lambda i: (0, 0)),
                  pl.BlockSpec((e, d), lambda i: (0, 0)),
                  pl.BlockSpec((e, 1), lambda i: (0, 0))],
        out_specs=[pl.BlockSpec((tm, d), lambda i: (i, 0)),
                   pl.BlockSpec((1, tm), lambda i: (0, i)),
                   pl.BlockSpec((2, tm), lambda i: (0, i))],
        out_shape=[_sds((rows, d), F32), _sds((1, rows), I32), _sds((2, rows), F32)],
        compiler_params=_params(("parallel",)),
        name="moe_router",
    )(x, mod3, mod3, norm_g.reshape(1, d), rw_t, rb.reshape(e, 1))


_PAIR_LO = np.array([0, 0, 0, 1, 1, 2], np.int32)
_PAIR_HI = np.array([1, 2, 3, 2, 3, 3], np.int32)
_CLS_LO = np.concatenate([EXP_PER_GROUP * g + _PAIR_LO for g in range(N_GROUPS)])
_CLS_HI = np.concatenate([EXP_PER_GROUP * g + _PAIR_HI for g in range(N_GROUPS)])


def _moe_plan(cls, wts, tm):
    t = cls.shape[0]
    n_tiles = t // tm + N_CLASSES
    r = n_tiles * tm
    perm = jnp.argsort(cls, stable=True).astype(I32)
    cls_s = cls[perm]
    cnt = jnp.sum((cls[:, None] == jnp.arange(N_CLASSES, dtype=I32)[None, :]).astype(I32), axis=0)
    cstart = jnp.cumsum(cnt) - cnt
    nt_c = (cnt + tm - 1) // tm
    t_end = jnp.cumsum(nt_c)
    t_off = t_end - nt_c
    pos = jnp.arange(t, dtype=I32)
    dest = t_off[cls_s] * tm + (pos - cstart[cls_s])
    src = jnp.zeros((r,), I32).at[dest].set(perm)
    dst = jnp.full((r,), -1, I32).at[dest].set(perm)
    wsort = jnp.zeros((r, 2), F32).at[dest].set(wts.T[perm])
    tile = jnp.arange(n_tiles, dtype=I32)
    total = t_end[-1]
    valid = (tile < total).astype(I32)
    tcls = jnp.searchsorted(t_end, tile, side="right").astype(I32)
    tcls = jnp.where(valid == 1, tcls, tcls[total - 1])
    par = jnp.where(valid == 1, (tile - t_off[tcls]) % 2, 0)
    lo = jnp.asarray(_CLS_LO)[tcls]
    hi = jnp.asarray(_CLS_HI)[tcls]
    e0 = jnp.where(par == 0, lo, hi)
    e1 = jnp.where(par == 0, hi, lo)
    step_e = jnp.stack([e0, e1], axis=1).reshape(-1)
    e_last = step_e[2 * (total - 1) + 1]
    step_e = jnp.where(jnp.repeat(valid, 2) == 1, step_e, e_last)
    return {"src": src.reshape(n_tiles, 1, tm), "dst": dst.reshape(n_tiles, 1, tm), "w": wsort,
            "step_e": step_e, "valid": valid, "flip": par, "n_tiles": n_tiles}


def _expert_kernel(tm, se_ref, va_ref, fl_ref, src_ref, dst_ref, wt_ref, h_hbm, w1_ref, w3_ref, w2_ref,
                   o_hbm, xg_ref, acc_ref, gsem, ssem):
    del se_ref
    i = pl.program_id(0)
    k = pl.program_id(1)

    def row_in(r):
        return pltpu.make_async_copy(h_hbm.at[pl.ds(src_ref[0, r], 1), :], xg_ref.at[pl.ds(r, 1), :], gsem)

    def row_out(r):
        return pltpu.make_async_copy(acc_ref.at[pl.ds(r, 1), :], o_hbm.at[pl.ds(dst_ref[0, r], 1), :], ssem)

    @pl.when(va_ref[i] == 1)
    def _():
        @pl.when(k == 0)
        def _():
            def start(r, c):
                row_in(r).start()
                return c
            lax.fori_loop(0, tm, start, 0)

            def wait(r, c):
                row_in(r).wait()
                return c
            lax.fori_loop(0, tm, wait, 0)

        x = xg_ref[...].astype(BF16)
        h1 = jnp.dot(x, w1_ref[...], preferred_element_type=F32)
        h3 = jnp.dot(x, w3_ref[...], preferred_element_type=F32)
        a = ((h1 * jax.nn.sigmoid(h1)) * h3).astype(BF16)
        y = jnp.dot(a, w2_ref[...], preferred_element_type=F32)
        wsel = (k + fl_ref[i]) % 2
        wt = jnp.where(wsel == 0, wt_ref[:, 0:1], wt_ref[:, 1:2])

        @pl.when(k == 0)
        def _():
            acc_ref[...] = wt * y

        @pl.when(k == 1)
        def _():
            acc_ref[...] = acc_ref[...] + wt * y

            def start(r, c):
                @pl.when(dst_ref[0, r] >= 0)
                def _():
                    row_out(r).start()
                return c
            lax.fori_loop(0, tm, start, 0)

            def wait(r, c):
                @pl.when(dst_ref[0, r] >= 0)
                def _():
                    row_out(r).wait()
                return c
            lax.fori_loop(0, tm, wait, 0)


def _expert_call(h2, plan, w1_bf, w3_bf, w2_bf, tm):
    rows, d = h2.shape
    de = w1_bf.shape[2]
    n_tiles = plan["n_tiles"]
    grid_spec = pltpu.PrefetchScalarGridSpec(
        num_scalar_prefetch=3,
        grid=(n_tiles, 2),
        in_specs=[pl.BlockSpec((None, 1, tm), lambda i, k, se, va, fl: (i, 0, 0), memory_space=pltpu.SMEM),
                  pl.BlockSpec((None, 1, tm), lambda i, k, se, va, fl: (i, 0, 0), memory_space=pltpu.SMEM),
                  pl.BlockSpec((tm, 2), lambda i, k, se, va, fl: (i, 0)),
                  pl.BlockSpec(memory_space=pl.ANY),
                  pl.BlockSpec((None, d, de), lambda i, k, se, va, fl: (se[2 * i + k], 0, 0)),
                  pl.BlockSpec((None, d, de), lambda i, k, se, va, fl: (se[2 * i + k], 0, 0)),
                  pl.BlockSpec((None, de, d), lambda i, k, se, va, fl: (se[2 * i + k], 0, 0))],
        out_specs=pl.BlockSpec(memory_space=pl.ANY),
        scratch_shapes=[pltpu.VMEM((tm, d), F32), pltpu.VMEM((tm, d), F32),
                        pltpu.SemaphoreType.DMA(()), pltpu.SemaphoreType.DMA(())])
    return pl.pallas_call(
        functools.partial(_expert_kernel, tm),
        grid_spec=grid_spec,
        out_shape=_sds((rows, d), F32),
        compiler_params=_params(("arbitrary", "arbitrary")),
        name="moe_experts",
    )(plan["step_e"], plan["valid"], plan["flip"], plan["src"], plan["dst"], plan["w"], h2,
      w1_bf, w3_bf, w2_bf)


def _final_kernel(x_ref, moe_ref, g2_ref, g_ref, o_ref):
    x = x_ref[...] + g2_ref[...] * moe_ref[...]
    ms = jnp.mean(x * x, axis=-1, keepdims=True)
    o_ref[...] = x * lax.rsqrt(ms + EPS) * g_ref[...]


def _final_call(x, moe, mod3, final_g, rows, seq_len, n_batch):
    d = x.shape[1]
    tm = _tile(math.gcd(seq_len, rows), 512)
    return pl.pallas_call(
        _final_kernel,
        grid=(rows // tm,),
        in_specs=[pl.BlockSpec((tm, d), lambda i: (i, 0)),
                  pl.BlockSpec((tm, d), lambda i: (i, 0)),
                  pl.BlockSpec((None, 1, d), lambda i: (jnp.minimum((i * tm) // seq_len, n_batch), 0, 5)),
                  pl.BlockSpec((1, d), lambda i: (0, 0))],
        out_specs=pl.BlockSpec((tm, d), lambda i: (i, 0)),
        out_shape=_sds((rows, d), F32),
        compiler_params=_params(("parallel",)),
        name="final_norm",
    )(x, moe, mod3, final_g.reshape(1, d))


def kernel(x, c, ctx, c_ctx, ada_w, ada_b, norm1_g, norm2_g, w_in, conv_a_w, conv_a_b, lru_wa, lru_ba, lru_wx, lru_bx, lru_lam, sgu_norm_g, sgu_ws, sgu_bs, hy_conv_w, hy_conv_b, hy_w1, hy_b1, hy_w2, hy_b2, hy_w3, hy_freq, hy_skip, w_br, w_out, exp_w1, exp_w3, exp_w2, router_w, router_b, final_g):
    n_batch, n_lat, d = x.shape
    n_ctx = ctx.shape[1]
    depth = ada_w.shape[0]
    w = d // 2
    t_lat = n_batch * n_lat
    t_ctx = n_batch * n_ctx
    t_all = t_lat + t_ctx
    heads, bw = lru_wa.shape[2], lru_wa.shape[3]
    groups = sgu_ws.shape[1]
    emb = hy_w1.shape[1]
    assert n_batch < MOD_ROWS and router_w.shape[1] == N_GROUPS * EXP_PER_GROUP
    assert n_lat % SGU_CHUNK == 0 and n_ctx % SGU_CHUNK == 0

    xs = jnp.concatenate([x.reshape(t_lat, d), ctx.reshape(t_ctx, d)], axis=0)
    c_all = jnp.zeros((MOD_ROWS, d), F32).at[:n_batch].set(c).at[n_batch].set(c_ctx)
    mod = _mod_all(c_all, ada_w, ada_b)

    tk_lat = _tile(n_lat, 256)
    tk_ctx = _tile(n_ctx, 256)
    consts_lat = _dft_consts(n_lat, w, emb, tk_lat)
    consts_ctx = _dft_consts(n_ctx, w, emb, tk_ctx)
    rw_t = router_w.T
    tm_e = 256

    moe = None
    mod3_prev = None
    for l in range(depth):
        last = l == depth - 1
        rows = t_lat if last else t_all
        mod3 = mod[l].reshape(MOD_ROWS, 1, 6 * d)
        z, xs = _win_call(xs, moe, mod3, mod3_prev, norm1_g[l], w_in[l].astype(BF16), t_all, n_lat, n_batch)

        lw = {"conv_w": conv_a_w[l], "conv_b": conv_a_b[l].reshape(1, w),
              "wg": jnp.concatenate([lru_wa[l], lru_wx[l]], axis=-1).astype(BF16),
              "bg": jnp.concatenate([lru_ba[l].reshape(2, heads, 1, bw), lru_bx[l].reshape(2, heads, 1, bw)], axis=-1),
              "lam": lru_lam[l].reshape(2, 1, w)}
        hlru = _lru_call(z, lw, n_lat, n_ctx, n_batch)

        bias_full = jnp.repeat(sgu_bs[l].T, w // groups, axis=1)
        yb = _sgu_call(z, sgu_norm_g[l], sgu_ws[l].astype(BF16), bias_full, rows)

        hw = {"conv_w": hy_conv_w[l], "conv_b": hy_conv_b[l],
              "w1": jnp.pad(hy_w1[l], ((0, 128 - emb), (0, 0))), "b1": hy_b1[l][None, :],
              "w2": hy_w2[l], "b2": hy_b2[l][None, :], "w3": hy_w3[l], "freq": hy_freq[l][None, :],
              "skip3": hy_skip[l].reshape(hy_skip.shape[1], 1, w)}
        yc_lat = _hyena_call(z, consts_lat, hw, 0, n_lat, n_batch)
        yc_ctx = yc_lat if last else _hyena_call(z, consts_ctx, hw, t_lat, n_ctx, n_batch)

        xs = _merge_call(hlru, z, yb, yc_lat, yc_ctx, xs, mod3, w_br[l].astype(BF16), w_out[l].astype(BF16),
                         rows, n_lat, n_batch)

        h2, cls, wts = _router_call(xs, mod3, norm2_g[l], rw_t, router_b, rows, n_lat, n_batch)
        plan = _moe_plan(cls[0], wts, tm_e)
        moe = _expert_call(h2, plan, exp_w1[l].astype(BF16), exp_w3[l].astype(BF16),
                           exp_w2[l].astype(BF16), tm_e)
        mod3_prev = mod3

    out = _final_call(xs, moe, mod3_prev, final_g, t_lat, n_lat, n_batch)
    return out.reshape(n_batch, n_lat, d)
```

```python
import functools
import math

import jax
import jax.numpy as jnp
import numpy as np
from jax import lax
from jax.experimental import pallas as pl
from jax.experimental.pallas import tpu as pltpu

F32 = jnp.float32
BF16 = jnp.bfloat16
I32 = jnp.int32

EPS = 1e-6
LRU_C = 8.0
SGU_CHUNK = 128
HY_TARGET = 1e-2
HY_FAST_PCT = 0.3
HY_SLOW_PCT = 1.5
HY_MAX_DECAY = math.log(HY_TARGET) / HY_FAST_PCT
HY_MIN_DECAY = math.log(HY_TARGET) / HY_SLOW_PCT
N_GROUPS = 4
EXP_PER_GROUP = 4
N_PAIRS = 6
N_CLASSES = N_GROUPS * N_PAIRS
MOD_ROWS = 16
V7X_VMEM_LIMIT = 56 * 1024 * 1024
PROLOGUE_ROWS = 128
HALO = 16
MOE_TILE = 512
DMA_UNROLL = 8


def _params(sem, vmem=V7X_VMEM_LIMIT):
    return pltpu.CompilerParams(dimension_semantics=sem, vmem_limit_bytes=vmem)


def _sds(shape, dtype):
    return jax.ShapeDtypeStruct(shape, dtype)


def _tile(n, pref):
    t = min(pref, n)
    while n % t:
        t //= 2
    return t


def _modulate(x, g, shift, scale):
    ms = jnp.mean(x * x, axis=-1, keepdims=True)
    y = x * lax.rsqrt(ms + EPS) * g
    return y * (1.0 + scale) + shift


def _mod_kernel(c_ref, w_ref, b_ref, o_ref):
    c = c_ref[...]
    s = (c * jax.nn.sigmoid(c)).astype(BF16)
    o_ref[...] = jnp.dot(s, w_ref[...].astype(BF16), preferred_element_type=F32) + b_ref[...]


def _mod_all(c_all, ada_w, ada_b):
    depth, d, n = ada_w.shape
    tn = _tile(n, 1024)
    return pl.pallas_call(
        _mod_kernel,
        grid=(depth, n // tn),
        in_specs=[pl.BlockSpec((MOD_ROWS, d), lambda l, j: (0, 0)),
                  pl.BlockSpec((None, d, tn), lambda l, j: (l, 0, j)),
                  pl.BlockSpec((None, 1, tn), lambda l, j: (l, 0, j))],
        out_specs=pl.BlockSpec((None, MOD_ROWS, tn), lambda l, j: (l, 0, j)),
        out_shape=_sds((depth, MOD_ROWS, n), F32),
        compiler_params=_params(("parallel", "parallel")),
        name="adaln_mod",
    )(c_all, ada_w, ada_b.reshape(depth, 1, n))


def _win_kernel(fuse_moe, *refs):
    if fuse_moe:
        x_ref, moe_ref, g2_ref, sh_ref, sc_ref, g_ref, w_ref, o_ref, h_ref = refs
    else:
        x_ref, sh_ref, sc_ref, g_ref, w_ref, o_ref, h_ref = refs

    @pl.when(pl.program_id(1) == 0)
    def _():
        def chunk(c, carry):
            rs = pl.ds(pl.multiple_of(c * PROLOGUE_ROWS, PROLOGUE_ROWS), PROLOGUE_ROWS)
            x = x_ref[rs, :]
            if fuse_moe:
                x = x + g2_ref[...] * moe_ref[rs, :]
            h_ref[rs, :] = _modulate(x, g_ref[...], sh_ref[...], sc_ref[...]).astype(BF16)
            return carry
        lax.fori_loop(0, x_ref.shape[0] // PROLOGUE_ROWS, chunk, 0)

    o_ref[...] = jnp.dot(h_ref[...], w_ref[...], preferred_element_type=F32).astype(o_ref.dtype)


def _win_call(x, moe, mod3, mod3_prev, norm_g, w_bf, rows, seq_len, n_batch):
    d = x.shape[1]
    n = w_bf.shape[1]
    tm = _tile(math.gcd(seq_len, rows), 1024)
    tn = d // 2
    fuse = moe is not None

    def mrow(i):
        return jnp.minimum((i * tm) // seq_len, n_batch)

    x_spec = pl.BlockSpec((tm, d), lambda i, j: (i, 0))

    def mod_spec(col):
        return pl.BlockSpec((None, 1, d), lambda i, j: (mrow(i), 0, col))

    in_specs = [x_spec]
    args = [x]
    if fuse:
        in_specs += [pl.BlockSpec((tm, d), lambda i, j: (i, 0), pipeline_mode=pl.Buffered(1)), mod_spec(5)]
        args += [moe, mod3_prev]
    in_specs += [mod_spec(0), mod_spec(1), pl.BlockSpec((1, d), lambda i, j: (0, 0)),
                 pl.BlockSpec((d, tn), lambda i, j: (0, j))]
    args += [mod3, mod3, norm_g.reshape(1, d), w_bf]
    return pl.pallas_call(
        functools.partial(_win_kernel, fuse),
        grid=(rows // tm, n // tn),
        in_specs=in_specs,
        out_specs=pl.BlockSpec((tm, tn), lambda i, j: (i, j)),
        out_shape=_sds((x.shape[0], n), BF16),
        scratch_shapes=[pltpu.VMEM((tm, d), BF16)],
        compiler_params=_params(("parallel", "arbitrary")),
        name="in_proj",
    )(*args)


def _lru_tile(d, j, nt_c, nt_l):
    is_ctx = j < nt_c
    nt = jnp.where(is_ctx, nt_c, nt_l)
    jl = jnp.where(is_ctx, j, j - nt_c)
    return is_ctx, jl + d * (nt - 1 - 2 * jl), nt


def _lru_kernel(tl, nt_c, nt_l, heads, xm_ref, xp_ref, xn_ref, cw_ref, cb_ref, wg_ref, bg_ref, lam_ref,
                o_ref, pad_ref, a_ref, b_ref, hs_ref, hc_ref):
    d = pl.program_id(0)
    j = pl.program_id(2)
    _, jt, nt = _lru_tile(d, j, nt_c, nt_l)
    w = a_ref.shape[1]
    bw = w // heads

    pad_ref[8:8 + tl, :] = xm_ref[...].astype(F32)
    prev = xp_ref[...].astype(F32)[HALO - 8:HALO]
    nxt = xn_ref[...].astype(F32)[0:8]
    pad_ref[0:8, :] = jnp.where(jt > 0, prev, 0.0)
    pad_ref[8 + tl:16 + tl, :] = jnp.where(jt < nt - 1, nxt, 0.0)
    cw = cw_ref[...]
    u = (cw[0:1] * pad_ref[7:7 + tl, :] + cw[1:2] * pad_ref[8:8 + tl, :]
         + cw[2:3] * pad_ref[9:9 + tl, :] + cw[3:4] * pad_ref[10:10 + tl, :]) + cb_ref[...]
    ub = u.astype(BF16)
    lam = lam_ref[...]
    sp = jnp.maximum(-lam, 0.0) + jnp.log1p(jnp.exp(-jnp.abs(lam)))
    for hd in range(heads):
        cs = slice(hd * bw, (hd + 1) * bw)
        g = jnp.dot(ub[:, cs], wg_ref[hd], preferred_element_type=F32) + bg_ref[hd]
        r = jax.nn.sigmoid(g[:, :bw])
        i = jax.nn.sigmoid(g[:, bw:])
        log_a = (-LRU_C) * r * sp[:, cs]
        a = jnp.exp(log_a)
        a_ref[:, cs] = a
        b_ref[:, cs] = jnp.sqrt(1.0 - a * a) * (i * u[:, cs])

    @pl.when(j == 0)
    def _():
        hc_ref[...] = jnp.zeros_like(hc_ref)

    def step(s, h):
        t = s + d * (tl - 1 - 2 * s)
        h = a_ref[pl.ds(t, 1), :] * h + b_ref[pl.ds(t, 1), :]
        hs_ref[pl.ds(t, 1), :] = h
        return h

    h = lax.fori_loop(0, tl, step, hc_ref[...], unroll=8)
    hc_ref[...] = h
    o_ref[...] = hs_ref[...].astype(o_ref.dtype)


def _lru_call(z, lw, n_lat, n_ctx, n_batch):
    w = lw["conv_w"].shape[1]
    heads = lw["wg"].shape[1]
    tl = _tile(math.gcd(n_lat, n_ctx), 512)
    nt_c, nt_l = n_ctx // tl, n_lat // tl
    ctx_blk0 = n_batch * nt_l
    per = tl // HALO
    last_h = n_batch * (n_lat + n_ctx) // HALO - 1

    def blk(d, b, j):
        is_ctx, jt, _ = _lru_tile(d, j, nt_c, nt_l)
        return jnp.where(is_ctx, ctx_blk0 + b * nt_c + jt, b * nt_l + jt)

    in_specs = [pl.BlockSpec((tl, w), lambda d, b, j: (blk(d, b, j), 0)),
                pl.BlockSpec((HALO, w), lambda d, b, j: (jnp.maximum(blk(d, b, j) * per - 1, 0), 0)),
                pl.BlockSpec((HALO, w), lambda d, b, j: (jnp.minimum((blk(d, b, j) + 1) * per, last_h), 0)),
                pl.BlockSpec((4, w), lambda d, b, j: (0, 0)),
                pl.BlockSpec((1, w), lambda d, b, j: (0, 0)),
                pl.BlockSpec((None, heads, w // heads, 2 * (w // heads)), lambda d, b, j: (d, 0, 0, 0)),
                pl.BlockSpec((None, heads, 1, 2 * (w // heads)), lambda d, b, j: (d, 0, 0, 0)),
                pl.BlockSpec((None, 1, w), lambda d, b, j: (d, 0, 0))]
    return pl.pallas_call(
        functools.partial(_lru_kernel, tl, nt_c, nt_l, heads),
        grid=(2, n_batch, nt_c + nt_l),
        in_specs=in_specs,
        out_specs=pl.BlockSpec((None, tl, w), lambda d, b, j: (d, blk(d, b, j), 0)),
        out_shape=_sds((2, n_batch * (n_lat + n_ctx), w), BF16),
        scratch_shapes=[pltpu.VMEM((tl + 16, w), F32), pltpu.VMEM((tl, w), F32),
                        pltpu.VMEM((tl, w), F32), pltpu.VMEM((tl, w), F32), pltpu.VMEM((1, w), F32)],
        compiler_params=_params(("arbitrary", "arbitrary", "arbitrary")),
        name="rglru_scan",
    )(z, z, z, lw["conv_w"], lw["conv_b"], lw["wg"], lw["bg"], lw["lam"])


def _sgu_kernel(groups, u_ref, v_ref, g_ref, ws_ref, bias_ref, o_ref):
    v = v_ref[...].astype(F32)
    mu = jnp.mean(v, axis=-1, keepdims=True)
    vc = v - mu
    var = jnp.mean(vc * vc, axis=-1, keepdims=True)
    vn = (vc * lax.rsqrt(var + EPS) * g_ref[...]).astype(BF16)
    ts, w = v.shape
    gw = w // groups
    for ck in range(ts // SGU_CHUNK):
        rs = slice(ck * SGU_CHUNK, (ck + 1) * SGU_CHUNK)
        for g in range(groups):
            cs = slice(g * gw, (g + 1) * gw)
            mixed = jnp.dot(ws_ref[g], vn[rs, cs], preferred_element_type=F32) + bias_ref[:, cs]
            o_ref[rs, cs] = (u_ref[rs, cs].astype(F32) * mixed).astype(o_ref.dtype)


def _sgu_call(z, norm_g, ws_bf, bias_full, rows):
    groups = ws_bf.shape[0]
    w = norm_g.shape[0]
    ts = _tile(rows, 512)
    return pl.pallas_call(
        functools.partial(_sgu_kernel, groups),
        grid=(rows // ts,),
        in_specs=[pl.BlockSpec((ts, w), lambda i: (i, 2)),
                  pl.BlockSpec((ts, w), lambda i: (i, 3)),
                  pl.BlockSpec((1, w), lambda i: (0, 0)),
                  pl.BlockSpec((groups, SGU_CHUNK, SGU_CHUNK), lambda i: (0, 0, 0)),
                  pl.BlockSpec((SGU_CHUNK, w), lambda i: (0, 0))],
        out_specs=pl.BlockSpec((ts, w), lambda i: (i, 0)),
        out_shape=_sds((rows, w), BF16),
        compiler_params=_params(("parallel",)),
        name="sgu",
    )(z, z, norm_g.reshape(1, w), ws_bf, bias_full)


def _hconv_kernel(tl, nt, xm_ref, xp_ref, xn_ref, cw_ref, cb_ref, o_ref, pad_ref):
    j = pl.program_id(2)
    pad_ref[8:8 + tl, :] = xm_ref[...].astype(F32)
    prev = xp_ref[...].astype(F32)[HALO - 8:HALO]
    nxt = xn_ref[...].astype(F32)[0:8]
    pad_ref[0:8, :] = jnp.where(j > 0, prev, 0.0)
    pad_ref[8 + tl:16 + tl, :] = jnp.where(j < nt - 1, nxt, 0.0)
    cw = cw_ref[...]
    u = (cw[0:1] * pad_ref[7:7 + tl, :] + cw[1:2] * pad_ref[8:8 + tl, :]
         + cw[2:3] * pad_ref[9:9 + tl, :]) + cb_ref[...]
    o_ref[...] = u.astype(o_ref.dtype)


def _hconv_call(z, conv_w, conv_b, row0, seq_len, n_batch):
    w = conv_w.shape[1] // 3
    tl = _tile(seq_len, 512)
    nt = seq_len // tl
    blk0 = row0 // tl
    hb0 = row0 // HALO
    per = tl // HALO
    last_h = (row0 + n_batch * seq_len) // HALO - 1
    return pl.pallas_call(
        functools.partial(_hconv_kernel, tl, nt),
        grid=(3, n_batch, nt),
        in_specs=[pl.BlockSpec((tl, w), lambda s, b, j: (blk0 + b * nt + j, 4 + s)),
                  pl.BlockSpec((HALO, w), lambda s, b, j: (jnp.maximum(hb0 + (b * nt + j) * per - 1, 0), 4 + s)),
                  pl.BlockSpec((HALO, w), lambda s, b, j: (jnp.minimum(hb0 + (b * nt + j + 1) * per, last_h), 4 + s)),
                  pl.BlockSpec((3, w), lambda s, b, j: (0, s)),
                  pl.BlockSpec((1, w), lambda s, b, j: (0, s))],
        out_specs=pl.BlockSpec((None, tl, w), lambda s, b, j: (s, b * nt + j, 0)),
        out_shape=_sds((3, n_batch * seq_len, w), BF16),
        scratch_shapes=[pltpu.VMEM((tl + 16, w), F32)],
        compiler_params=_params(("parallel", "parallel", "arbitrary")),
        name="hyena_conv",
    )(z, z, z, conv_w, conv_b.reshape(1, 3 * w))


def _hfilt_kernel(feat_ref, t_ref, dl_ref, w1_ref, b1_ref, w2_ref, b2_ref, w3_ref, fr_ref, o_ref):
    hi = lax.Precision.HIGHEST
    fr = fr_ref[...]
    h = jnp.sin(fr * (jnp.dot(feat_ref[...], w1_ref[...], precision=hi, preferred_element_type=F32) + b1_ref[...]))
    h = jnp.sin(fr * (jnp.dot(h, w2_ref[...], precision=hi, preferred_element_type=F32) + b2_ref[...]))
    h = jnp.dot(h, w3_ref[...], precision=hi, preferred_element_type=F32)
    w = h.shape[1] // 2
    h = h * jnp.exp(-t_ref[...] * dl_ref[...])
    fwd = h[:, :w]
    row = lax.broadcasted_iota(I32, (h.shape[0], 1), 0) + pl.program_id(1) * h.shape[0]
    bwd = jnp.where(row > 0, h[:, w:], 0.0)
    o_ref[:, :w] = (fwd + bwd).astype(o_ref.dtype)
    o_ref[:, w:] = (fwd - bwd).astype(o_ref.dtype)


def _hfilt_call(consts, hw):
    feat, tcol, deltas = consts["feat"], consts["t"], consts["deltas"]
    seq_len, fpad = feat.shape
    hid = hw["w2"].shape[0]
    w = deltas.shape[1]
    orders = hw["w3"].shape[1] // (2 * w)
    tr = _tile(seq_len, 512)
    return pl.pallas_call(
        _hfilt_kernel,
        grid=(orders, seq_len // tr),
        in_specs=[pl.BlockSpec((tr, fpad), lambda o, i: (i, 0)),
                  pl.BlockSpec((tr, 1), lambda o, i: (i, 0)),
                  pl.BlockSpec((1, 2 * w), lambda o, i: (0, 0)),
                  pl.BlockSpec((fpad, hid), lambda o, i: (0, 0)),
                  pl.BlockSpec((1, hid), lambda o, i: (0, 0)),
                  pl.BlockSpec((hid, hid), lambda o, i: (0, 0)),
                  pl.BlockSpec((1, hid), lambda o, i: (0, 0)),
                  pl.BlockSpec((hid, 2 * w), lambda o, i: (0, o)),
                  pl.BlockSpec((1, hid), lambda o, i: (0, 0))],
        out_specs=pl.BlockSpec((None, tr, 2 * w), lambda o, i: (o, i, 0)),
        out_shape=_sds((orders, seq_len, 2 * w), BF16),
        compiler_params=_params(("parallel", "parallel")),
        name="hyena_filter",
    )(feat, tcol, jnp.concatenate([deltas, deltas], axis=1), hw["w1"], hw["b1"], hw["w2"], hw["b2"],
      hw["w3"], hw["freq"])


def _hspec_kernel(tk, g_ref, p_ref, q_ref, ma_ref, mb_ref):
    re = jnp.dot(g_ref[:tk, :], p_ref[...], preferred_element_type=F32)
    kfi = jnp.dot(g_ref[tk:, :], q_ref[...], preferred_element_type=F32)
    first = (lax.broadcasted_iota(I32, (tk, 1), 0) == 0) & (pl.program_id(2) == 0)
    ma_ref[:tk, :] = re
    ma_ref[tk:, :] = re
    kfi = jnp.where(first, 0.0, kfi)
    mb_ref[:tk, :] = -kfi
    mb_ref[tk:, :] = kfi

    @pl.when(pl.program_id(2) == 0)
    def _():
        nyq = jnp.dot(g_ref[tk:tk + HALO, :], p_ref[...], preferred_element_type=F32)
        row0 = lax.broadcasted_iota(I32, (HALO, 1), 0) == 0
        ma_ref[tk:tk + HALO, :] = jnp.where(row0, nyq, re[:HALO])


def _hspec_call(gf, pq, tk):
    orders, seq_len, w2 = pq.shape
    w = w2 // 2
    n2 = gf.shape[0]
    tn = _tile(w, 512)
    nn = w // tn
    return pl.pallas_call(
        functools.partial(_hspec_kernel, tk),
        grid=(orders, nn, n2 // (2 * tk)),
        in_specs=[pl.BlockSpec((2 * tk, seq_len), lambda o, n, m: (m, 0)),
                  pl.BlockSpec((None, seq_len, tn), lambda o, n, m: (o, 0, n)),
                  pl.BlockSpec((None, seq_len, tn), lambda o, n, m: (o, 0, nn + n))],
        out_specs=[pl.BlockSpec((None, 2 * tk, tn), lambda o, n, m: (o, m, n)),
                   pl.BlockSpec((None, 2 * tk, tn), lambda o, n, m: (o, m, n))],
        out_shape=[_sds((orders, n2, w), F32), _sds((orders, n2, w), F32)],
        compiler_params=_params(("parallel", "parallel", "arbitrary")),
        name="hyena_filter_spectrum",
    )(gf, pq, pq)


def _hfwd_kernel(tk, g_ref, x_ref, ma_ref, mb_ref, o_ref):
    s = jnp.dot(g_ref[...], x_ref[...], preferred_element_type=F32)
    sw = jnp.concatenate([s[tk:], s[:tk]], axis=0)
    o_ref[...] = (s * ma_ref[...] + sw * mb_ref[...]).astype(o_ref.dtype)


def _hfwd_call(gf, x2d, xblk0, ma, mb, order, tk, n_batch):
    n2, seq_len = gf.shape
    w = x2d.shape[1]
    return pl.pallas_call(
        functools.partial(_hfwd_kernel, tk),
        grid=(n2 // (2 * tk), n_batch),
        in_specs=[pl.BlockSpec((2 * tk, seq_len), lambda m, b: (m, 0)),
                  pl.BlockSpec((seq_len, w), lambda m, b: (xblk0 + b, 0)),
                  pl.BlockSpec((None, 2 * tk, w), lambda m, b: (order, m, 0)),
                  pl.BlockSpec((None, 2 * tk, w), lambda m, b: (order, m, 0))],
        out_specs=pl.BlockSpec((None, 2 * tk, w), lambda m, b: (b, m, 0)),
        out_shape=_sds((n_batch, n2, w), BF16),
        compiler_params=_params(("parallel", "arbitrary")),
        name="hyena_dft",
    )(gf, x2d, ma, mb)


def _hinv_kernel(g_ref, p_ref, v_ref, x_ref, sk_ref, o_ref):
    y = jnp.dot(g_ref[...], p_ref[...], preferred_element_type=F32)
    v = v_ref[...].astype(F32)
    o_ref[...] = (x_ref[...].astype(F32) * (y + v * sk_ref[...])).astype(o_ref.dtype)


def _hinv_call(gi, p, v2d, vblk0, x2d, xblk0, skip3, order, n_batch):
    seq_len, n2 = gi.shape
    w = v2d.shape[1]
    tm = _tile(seq_len, 512)
    tn = _tile(w, 512)
    nm = seq_len // tm
    return pl.pallas_call(
        _hinv_kernel,
        grid=(nm, n_batch, w // tn),
        in_specs=[pl.BlockSpec((tm, n2), lambda m, b, n: (m, 0)),
                  pl.BlockSpec((None, n2, tn), lambda m, b, n: (b, 0, n)),
                  pl.BlockSpec((tm, tn), lambda m, b, n: (vblk0 + b * nm + m, n)),
                  pl.BlockSpec((tm, tn), lambda m, b, n: (xblk0 + b * nm + m, n)),
                  pl.BlockSpec((None, 1, tn), lambda m, b, n: (order, 0, n))],
        out_specs=pl.BlockSpec((tm, tn), lambda m, b, n: (b * nm + m, n)),
        out_shape=_sds((n_batch * seq_len, w), BF16),
        compiler_params=_params(("parallel", "arbitrary", "arbitrary")),
        name="hyena_idft",
    )(gi, p, v2d, x2d, skip3)


def _dft_consts(seq_len, w, emb, tk):
    f32 = F32
    k = jnp.arange(seq_len, dtype=I32)
    nb = seq_len // tk

    def cs_table(rows):
        ang = ((rows[:, None] * k[None, :]) % (2 * seq_len)).astype(f32) * (math.pi / seq_len)
        return jnp.cos(ang), jnp.sin(ang)

    ca, sa = cs_table(jnp.arange(nb, dtype=I32) * tk)
    cb, sb = cs_table(jnp.arange(tk, dtype=I32))
    c = (ca[:, None, :] * cb[None, :, :] - sa[:, None, :] * sb[None, :, :]).reshape(seq_len, seq_len)
    s = (sa[:, None, :] * cb[None, :, :] + ca[:, None, :] * sb[None, :, :]).reshape(seq_len, seq_len)
    nyq = jnp.where(k % 2 == 0, 1.0, -1.0).astype(f32)
    im_f = (-s).at[0].set(nyq)
    gf = jnp.stack([c.reshape(nb, tk, seq_len), im_f.reshape(nb, tk, seq_len)], axis=1)
    gf = gf.reshape(2 * seq_len, seq_len).astype(BF16)
    ci = (c / seq_len).at[:, 0].multiply(0.5)
    si = (-s / seq_len).at[:, 0].set(nyq / (2 * seq_len))
    gi = jnp.stack([ci.reshape(seq_len, nb, tk), si.reshape(seq_len, nb, tk)], axis=2)
    gi = gi.reshape(seq_len, 2 * seq_len).astype(BF16)
    bands = (emb - 1) // 2
    t = jnp.linspace(0.0, 1.0, seq_len, dtype=f32)[:, None]
    wv = 2.0 * math.pi * jnp.arange(seq_len, dtype=f32)[:, None] / seq_len
    fb = jnp.linspace(1e-4, bands - 1, bands, dtype=f32)[None, :]
    feat = jnp.concatenate([t, jnp.cos(fb * wv), -jnp.sin(fb * wv)], axis=-1)
    fpad = 128
    feat = jnp.pad(feat, ((0, 0), (0, fpad - emb)))
    deltas = jnp.abs(jnp.linspace(HY_MIN_DECAY, HY_MAX_DECAY, w, dtype=f32))[None, :]
    return {"gf": gf, "gi": gi, "feat": feat, "t": t, "deltas": deltas, "tk": tk}


def _hyena_call(z, consts, hw, row0, seq_len, n_batch):
    tk = consts["tk"]
    w = hw["skip3"].shape[2]
    hc = _hconv_call(z, hw["conv_w"], hw["conv_b"], row0, seq_len, n_batch)
    hc2 = hc.reshape(3 * n_batch * seq_len, w)
    pq = _hfilt_call(consts, hw)
    ma, mb = _hspec_call(consts["gf"], pq, tk)
    tm = _tile(seq_len, 512)
    sec = n_batch * seq_len // tm
    p0 = _hfwd_call(consts["gf"], hc2, 0, ma, mb, 0, tk, n_batch)
    s1 = _hinv_call(consts["gi"], p0, hc2, 0, hc2, sec, hw["skip3"], 0, n_batch)
    p1 = _hfwd_call(consts["gf"], s1, 0, ma, mb, 1, tk, n_batch)
    return _hinv_call(consts["gi"], p1, s1, 0, hc2, 2 * sec, hw["skip3"], 1, n_batch)


def _merge_kernel(lat_tiles, fuse_moe, hf_ref, hb_ref, ga_ref, yb_ref, ycl_ref, ycc_ref, g0a, g0b, g1a, g1b, g2a, g2b,
                  x_ref, gate_ref, wbr_ref, wout_ref, *rest):
    if fuse_moe:
        moe_ref, mg_ref, o_ref, m_ref = rest
    else:
        o_ref, m_ref = rest
    ga = ga_ref[...].astype(F32)
    ya = ((hf_ref[...].astype(F32) + hb_ref[...].astype(F32)) * jax.nn.gelu(ga)).astype(BF16)
    yc = jnp.where(pl.program_id(0) < lat_tiles, ycl_ref[...], ycc_ref[...])
    ys = (ya, yb_ref[...], yc)
    w = ya.shape[1]
    gates = ((g0a, g0b), (g1a, g1b), (g2a, g2b))
    for half in range(2):
        cs = slice(half * w, (half + 1) * w)
        acc = None
        for br in range(3):
            gt = jax.nn.sigmoid(gates[br][half][...].astype(F32))
            t = gt * jnp.dot(ys[br], wbr_ref[br, :, cs], preferred_element_type=F32)
            acc = t if acc is None else acc + t
        m_ref[:, cs] = acc.astype(BF16)
    out = jnp.dot(m_ref[...], wout_ref[...], preferred_element_type=F32)
    x = x_ref[...]
    if fuse_moe:
        x = x + mg_ref[...] * moe_ref[...]
    o_ref[...] = x + gate_ref[...] * out


def _merge_call(hlru, z, yb, yc_lat, yc_ctx, x, moe, mod3, mod3_prev, wbr_bf, wout_bf, rows, seq_len, n_batch):
    d = x.shape[1]
    fuse = moe is not None
    w = d // 2
    tm = _tile(math.gcd(seq_len, rows), 256)
    lat_tiles = yc_lat.shape[0] // tm
    ctx_last = yc_ctx.shape[0] // tm - 1

    def mrow(i):
        return jnp.minimum((i * tm) // seq_len, n_batch)

    def zcol(c):
        return pl.BlockSpec((tm, w), lambda i: (i, c))

    const = dict(pipeline_mode=pl.Buffered(1))
    in_specs = [pl.BlockSpec((None, tm, w), lambda i: (0, i, 0)),
                pl.BlockSpec((None, tm, w), lambda i: (1, i, 0)),
                zcol(1), zcol(0),
                pl.BlockSpec((tm, w), lambda i: (jnp.minimum(i, lat_tiles - 1), 0)),
                pl.BlockSpec((tm, w), lambda i: (jnp.clip(i - lat_tiles, 0, ctx_last), 0))] + [
                zcol(7 + c) for c in range(6)] + [
                pl.BlockSpec((tm, d), lambda i: (i, 0)),
                pl.BlockSpec((None, 1, d), lambda i: (mrow(i), 0, 2)),
                pl.BlockSpec((3, w, d), lambda i: (0, 0, 0), **const),
                pl.BlockSpec((d, d), lambda i: (0, 0), **const)]
    args = [hlru, hlru, z, yb, yc_lat, yc_ctx, z, z, z, z, z, z, x, mod3, wbr_bf, wout_bf]
    if fuse:
        in_specs += [pl.BlockSpec((tm, d), lambda i: (i, 0)),
                     pl.BlockSpec((None, 1, d), lambda i: (mrow(i), 0, 5))]
        args += [moe, mod3_prev]
    return pl.pallas_call(
        functools.partial(_merge_kernel, lat_tiles, fuse),
        grid=(rows // tm,),
        in_specs=in_specs,
        out_specs=pl.BlockSpec((tm, d), lambda i: (i, 0)),
        out_shape=_sds((rows, d), F32),
        scratch_shapes=[pltpu.VMEM((tm, d), BF16)],
        compiler_params=_params(("parallel",)),
        name="merge_out_proj",
    )(*args)


def _router_kernel(x_ref, sh_ref, sc_ref, g_ref, rw_ref, rb_ref, h_ref, cls_ref, wt_ref):
    h = _modulate(x_ref[...], g_ref[...], sh_ref[...], sc_ref[...])
    h_ref[...] = h
    logits = lax.dot_general(rw_ref[...], h, (((1,), (1,)), ((), ())),
                             precision=lax.Precision.HIGHEST, preferred_element_type=F32)
    s = jax.nn.sigmoid(logits)
    sb = s + rb_ref[...]
    rows_b = [sb[e:e + 1, :] for e in range(N_GROUPS * EXP_PER_GROUP)]
    rows_s = [s[e:e + 1, :] for e in range(N_GROUPS * EXP_PER_GROUP)]
    gscore = []
    for g in range(N_GROUPS):
        v = rows_b[EXP_PER_GROUP * g:EXP_PER_GROUP * (g + 1)]
        best = None
        for a in range(EXP_PER_GROUP):
            for b in range(a + 1, EXP_PER_GROUP):
                pr = v[a] + v[b]
                best = pr if best is None else jnp.maximum(best, pr)
        gscore.append(best)
    best = gscore[0]
    gsel = jnp.zeros(best.shape, I32)
    for g in range(1, N_GROUPS):
        better = gscore[g] > best
        best = jnp.where(better, gscore[g], best)
        gsel = jnp.where(better, g, gsel)

    def pick(rows, i):
        out = rows[i]
        for g in range(1, N_GROUPS):
            out = jnp.where(gsel == g, rows[EXP_PER_GROUP * g + i], out)
        return out

    vb = [pick(rows_b, i) for i in range(EXP_PER_GROUP)]
    vs = [pick(rows_s, i) for i in range(EXP_PER_GROUP)]
    sel = []
    for i in range(EXP_PER_GROUP):
        rank = jnp.zeros(best.shape, I32)
        for jx in range(EXP_PER_GROUP):
            if jx == i:
                continue
            ahead = (vb[jx] > vb[i]) if jx > i else (vb[jx] >= vb[i])
            rank = rank + ahead.astype(I32)
        sel.append(rank < 2)
    j_hi = jnp.where(sel[3], 3, jnp.where(sel[2], 2, 1))
    w_lo = jnp.where(sel[0], vs[0], jnp.where(sel[1], vs[1], vs[2]))
    w_hi = jnp.where(sel[3], vs[3], jnp.where(sel[2], vs[2], vs[1]))
    den = w_lo + w_hi
    pair = jnp.where(sel[0], j_hi - 1, jnp.where(sel[1], j_hi + 1, 5))
    cls_ref[...] = gsel * N_PAIRS + pair
    wt_ref[0:1, :] = w_lo / den
    wt_ref[1:2, :] = w_hi / den


def _router_call(x, mod3, norm_g, rw_t, rb, rows, seq_len, n_batch):
    d = x.shape[1]
    e = rw_t.shape[0]
    tm = _tile(math.gcd(seq_len, rows), 512)

    def mrow(i):
        return jnp.minimum((i * tm) // seq_len, n_batch)

    return pl.pallas_call(
        _router_kernel,
        grid=(rows // tm,),
        in_specs=[pl.BlockSpec((tm, d), lambda i: (i, 0)),
                  pl.BlockSpec((None, 1, d), lambda i: (mrow(i), 0, 3)),
                  pl.BlockSpec((None, 1, d), lambda i: (mrow(i), 0, 4)),
                  pl.BlockSpec((1, d), lambda i: (0, 0)),
                  pl.BlockSpec((e, d), lambda i: (0, 0)),
                  pl.BlockSpec((e, 1), lambda i: (0, 0))],
        out_specs=[pl.BlockSpec((tm, d), lambda i: (i, 0)),
                   pl.BlockSpec((1, tm), lambda i: (0, i)),
                   pl.BlockSpec((2, tm), lambda i: (0, i))],
        out_shape=[_sds((rows, d), F32), _sds((1, rows), I32), _sds((2, rows), F32)],
        compiler_params=_params(("parallel",)),
        name="moe_router",
    )(x, mod3, mod3, norm_g.reshape(1, d), rw_t, rb.reshape(e, 1))


_PAIR_LO = np.array([0, 0, 0, 1, 1, 2], np.int32)
_PAIR_HI = np.array([1, 2, 3, 2, 3, 3], np.int32)
_CLS_LO = np.concatenate([EXP_PER_GROUP * g + _PAIR_LO for g in range(N_GROUPS)])
_CLS_HI = np.concatenate([EXP_PER_GROUP * g + _PAIR_HI for g in range(N_GROUPS)])


def _moe_plan(cls, wts, tm):
    t = cls.shape[0]
    n_tiles = t // tm + N_CLASSES
    r = n_tiles * tm
    perm = jnp.argsort(cls, stable=True).astype(I32)
    cnt = jnp.sum((cls[:, None] == jnp.arange(N_CLASSES, dtype=I32)[None, :]).astype(I32), axis=0)
    cstart = jnp.cumsum(cnt) - cnt
    nt_c = (cnt + tm - 1) // tm
    t_end = jnp.cumsum(nt_c)
    t_off = t_end - nt_c
    tile = jnp.arange(n_tiles, dtype=I32)
    total = t_end[-1]
    valid = (tile < total).astype(I32)
    tcls = jnp.sum((t_end[None, :] <= jnp.minimum(tile, total - 1)[:, None]).astype(I32), axis=1)
    r_off = ((tile - t_off[tcls]) * tm)[:, None] + jnp.arange(tm, dtype=I32)[None, :]
    r_valid = (valid[:, None] == 1) & (r_off < cnt[tcls][:, None])
    src = jnp.where(r_valid, perm[jnp.clip(cstart[tcls][:, None] + r_off, 0, t - 1)], 0)
    dst = jnp.where(r_valid, src, -1)
    wsort = jnp.where(r_valid[:, :, None], wts.T[src], 0.0).reshape(r, 2)
    par = jnp.where(valid == 1, (tile - t_off[tcls]) % 2, 0)
    lo = jnp.asarray(_CLS_LO)[tcls]
    hi = jnp.asarray(_CLS_HI)[tcls]
    e0 = jnp.where(par == 0, lo, hi)
    e1 = jnp.where(par == 0, hi, lo)
    step_e = jnp.stack([e0, e1], axis=1).reshape(-1)
    e_last = step_e[2 * (total - 1) + 1]
    step_e = jnp.where(jnp.repeat(valid, 2) == 1, step_e, e_last)
    return {"src": src.reshape(n_tiles, 1, tm), "dst": dst.reshape(n_tiles, 1, tm), "w": wsort,
            "step_e": step_e, "valid": valid, "flip": par, "n_tiles": n_tiles}


def _expert_kernel(tm, se_ref, va_ref, fl_ref, src_ref, dst_ref, wt_ref, h_hbm, w1_ref, w3_ref, w2_ref,
                   o_hbm, xg_ref, xb_ref, acc_ref, gsem, ssem):
    del se_ref
    i = pl.program_id(0)
    k = pl.program_id(1)

    def row_in(r):
        return pltpu.make_async_copy(h_hbm.at[pl.ds(src_ref[0, r], 1), :], xg_ref.at[pl.ds(r, 1), :], gsem)

    def row_out(r):
        return pltpu.make_async_copy(acc_ref.at[pl.ds(r, 1), :], o_hbm.at[pl.ds(dst_ref[0, r], 1), :], ssem)

    @pl.when(va_ref[i] == 1)
    def _():
        @pl.when(k == 0)
        def _():
            def start(r, c):
                row_in(r).start()
                return c
            lax.fori_loop(0, tm, start, 0, unroll=DMA_UNROLL)

            def wait(r, c):
                row_in(r).wait()
                return c
            lax.fori_loop(0, tm, wait, 0, unroll=DMA_UNROLL)
            xb_ref[...] = xg_ref[...].astype(BF16)

        x = xb_ref[...]
        h1 = jnp.dot(x, w1_ref[...], preferred_element_type=F32)
        h3 = jnp.dot(x, w3_ref[...], preferred_element_type=F32)
        a = ((h1 * jax.nn.sigmoid(h1)) * h3).astype(BF16)
        y = jnp.dot(a, w2_ref[...], preferred_element_type=F32)
        wsel = (k + fl_ref[i]) % 2
        wt = jnp.where(wsel == 0, wt_ref[:, 0:1], wt_ref[:, 1:2])

        @pl.when(k == 0)
        def _():
            acc_ref[...] = wt * y

        @pl.when(k == 1)
        def _():
            acc_ref[...] = acc_ref[...] + wt * y

            full = dst_ref[0, tm - 1] >= 0

            @pl.when(full)
            def _():
                def start(r, c):
                    row_out(r).start()
                    return c
                lax.fori_loop(0, tm, start, 0, unroll=DMA_UNROLL)

                def wait(r, c):
                    row_out(r).wait()
                    return c
                lax.fori_loop(0, tm, wait, 0, unroll=DMA_UNROLL)

            @pl.when(jnp.logical_not(full))
            def _():
                def start(r, c):
                    @pl.when(dst_ref[0, r] >= 0)
                    def _():
                        row_out(r).start()
                    return c
                lax.fori_loop(0, tm, start, 0, unroll=DMA_UNROLL)

                def wait(r, c):
                    @pl.when(dst_ref[0, r] >= 0)
                    def _():
                        row_out(r).wait()
                    return c
                lax.fori_loop(0, tm, wait, 0, unroll=DMA_UNROLL)


def _expert_call(h2, plan, w1_bf, w3_bf, w2_bf, tm):
    rows, d = h2.shape
    de = w1_bf.shape[2]
    n_tiles = plan["n_tiles"]
    grid_spec = pltpu.PrefetchScalarGridSpec(
        num_scalar_prefetch=3,
        grid=(n_tiles, 2),
        in_specs=[pl.BlockSpec((None, 1, tm), lambda i, k, se, va, fl: (i, 0, 0), memory_space=pltpu.SMEM),
                  pl.BlockSpec((None, 1, tm), lambda i, k, se, va, fl: (i, 0, 0), memory_space=pltpu.SMEM),
                  pl.BlockSpec((tm, 2), lambda i, k, se, va, fl: (i, 0)),
                  pl.BlockSpec(memory_space=pl.ANY),
                  pl.BlockSpec((None, d, de), lambda i, k, se, va, fl: (se[2 * i + k], 0, 0)),
                  pl.BlockSpec((None, d, de), lambda i, k, se, va, fl: (se[2 * i + k], 0, 0)),
                  pl.BlockSpec((None, de, d), lambda i, k, se, va, fl: (se[2 * i + k], 0, 0))],
        out_specs=pl.BlockSpec(memory_space=pl.ANY),
        scratch_shapes=[pltpu.VMEM((tm, d), F32), pltpu.VMEM((tm, d), BF16), pltpu.VMEM((tm, d), F32),
                        pltpu.SemaphoreType.DMA(()), pltpu.SemaphoreType.DMA(())])
    return pl.pallas_call(
        functools.partial(_expert_kernel, tm),
        grid_spec=grid_spec,
        out_shape=_sds((rows, d), F32),
        compiler_params=pltpu.CompilerParams(dimension_semantics=("arbitrary", "arbitrary"),
                                             vmem_limit_bytes=V7X_VMEM_LIMIT, disable_bounds_checks=True),
        name="moe_experts",
    )(plan["step_e"], plan["valid"], plan["flip"], plan["src"], plan["dst"], plan["w"], h2,
      w1_bf, w3_bf, w2_bf)


def _final_kernel(x_ref, moe_ref, g2_ref, g_ref, o_ref):
    x = x_ref[...] + g2_ref[...] * moe_ref[...]
    ms = jnp.mean(x * x, axis=-1, keepdims=True)
    o_ref[...] = x * lax.rsqrt(ms + EPS) * g_ref[...]


def _final_call(x, moe, mod3, final_g, rows, seq_len, n_batch):
    d = x.shape[1]
    tm = _tile(math.gcd(seq_len, rows), 512)
    return pl.pallas_call(
        _final_kernel,
        grid=(rows // tm,),
        in_specs=[pl.BlockSpec((tm, d), lambda i: (i, 0)),
                  pl.BlockSpec((tm, d), lambda i: (i, 0)),
                  pl.BlockSpec((None, 1, d), lambda i: (jnp.minimum((i * tm) // seq_len, n_batch), 0, 5)),
                  pl.BlockSpec((1, d), lambda i: (0, 0))],
        out_specs=pl.BlockSpec((tm, d), lambda i: (i, 0)),
        out_shape=_sds((rows, d), F32),
        compiler_params=_params(("parallel",)),
        name="final_norm",
    )(x, moe, mod3, final_g.reshape(1, d))


def kernel(x, c, ctx, c_ctx, ada_w, ada_b, norm1_g, norm2_g, w_in, conv_a_w, conv_a_b, lru_wa, lru_ba, lru_wx, lru_bx, lru_lam, sgu_norm_g, sgu_ws, sgu_bs, hy_conv_w, hy_conv_b, hy_w1, hy_b1, hy_w2, hy_b2, hy_w3, hy_freq, hy_skip, w_br, w_out, exp_w1, exp_w3, exp_w2, router_w, router_b, final_g):
    n_batch, n_lat, d = x.shape
    n_ctx = ctx.shape[1]
    depth = ada_w.shape[0]
    w = d // 2
    t_lat = n_batch * n_lat
    t_ctx = n_batch * n_ctx
    t_all = t_lat + t_ctx
    heads, bw = lru_wa.shape[2], lru_wa.shape[3]
    groups = sgu_ws.shape[1]
    emb = hy_w1.shape[1]
    assert n_batch < MOD_ROWS and router_w.shape[1] == N_GROUPS * EXP_PER_GROUP
    assert n_lat % SGU_CHUNK == 0 and n_ctx % SGU_CHUNK == 0

    xs = jnp.concatenate([x.reshape(t_lat, d), ctx.reshape(t_ctx, d)], axis=0)
    c_all = jnp.zeros((MOD_ROWS, d), F32).at[:n_batch].set(c).at[n_batch].set(c_ctx)
    mod = _mod_all(c_all, ada_w, ada_b)

    tk_lat = _tile(n_lat, 256)
    tk_ctx = _tile(n_ctx, 256)
    consts_lat = _dft_consts(n_lat, w, emb, tk_lat)
    consts_ctx = _dft_consts(n_ctx, w, emb, tk_ctx)
    rw_t = router_w.T
    tm_e = _tile(math.gcd(t_lat, t_ctx), MOE_TILE)

    moe = None
    mod3_prev = None
    for l in range(depth):
        last = l == depth - 1
        rows = t_lat if last else t_all
        mod3 = mod[l].reshape(MOD_ROWS, 1, 6 * d)
        z = _win_call(xs, moe, mod3, mod3_prev, norm1_g[l], w_in[l].astype(BF16), t_all, n_lat, n_batch)

        lw = {"conv_w": conv_a_w[l], "conv_b": conv_a_b[l].reshape(1, w),
              "wg": jnp.concatenate([lru_wa[l], lru_wx[l]], axis=-1).astype(BF16),
              "bg": jnp.concatenate([lru_ba[l].reshape(2, heads, 1, bw), lru_bx[l].reshape(2, heads, 1, bw)], axis=-1),
              "lam": lru_lam[l].reshape(2, 1, w)}
        hlru = _lru_call(z, lw, n_lat, n_ctx, n_batch)

        bias_full = jnp.repeat(sgu_bs[l].T, w // groups, axis=1)
        yb = _sgu_call(z, sgu_norm_g[l], sgu_ws[l].astype(BF16), bias_full, rows)

        hw = {"conv_w": hy_conv_w[l], "conv_b": hy_conv_b[l],
              "w1": jnp.pad(hy_w1[l], ((0, 128 - emb), (0, 0))), "b1": hy_b1[l][None, :],
              "w2": hy_w2[l], "b2": hy_b2[l][None, :], "w3": hy_w3[l], "freq": hy_freq[l][None, :],
              "skip3": hy_skip[l].reshape(hy_skip.shape[1], 1, w)}
        yc_lat = _hyena_call(z, consts_lat, hw, 0, n_lat, n_batch)
        yc_ctx = yc_lat if last else _hyena_call(z, consts_ctx, hw, t_lat, n_ctx, n_batch)

        xs = _merge_call(hlru, z, yb, yc_lat, yc_ctx, xs, moe, mod3, mod3_prev, w_br[l].astype(BF16),
                         w_out[l].astype(BF16), rows, n_lat, n_batch)

        h2, cls, wts = _router_call(xs, mod3, norm2_g[l], rw_t, router_b, rows, n_lat, n_batch)
        plan = _moe_plan(cls[0], wts, tm_e)
        moe = _expert_call(h2, plan, exp_w1[l].astype(BF16), exp_w3[l].astype(BF16),
                           exp_w2[l].astype(BF16), tm_e)
        mod3_prev = mod3

    out = _final_call(xs, moe, mod3_prev, final_g, t_lat, n_lat, n_batch)
    return out.reshape(n_batch, n_lat, d)
```

```python
import functools
import math

import jax
import jax.numpy as jnp
import numpy as np
from jax import lax
from jax.experimental import pallas as pl
from jax.experimental.pallas import tpu as pltpu

F32 = jnp.float32
BF16 = jnp.bfloat16
I32 = jnp.int32

EPS = 1e-6
LRU_C = 8.0
SGU_CHUNK = 128
HY_TARGET = 1e-2
HY_FAST_PCT = 0.3
HY_SLOW_PCT = 1.5
HY_MAX_DECAY = math.log(HY_TARGET) / HY_FAST_PCT
HY_MIN_DECAY = math.log(HY_TARGET) / HY_SLOW_PCT
N_GROUPS = 4
EXP_PER_GROUP = 4
N_PAIRS = 6
N_CLASSES = N_GROUPS * N_PAIRS
MOD_ROWS = 16
V7X_VMEM_LIMIT = 56 * 1024 * 1024
PROLOGUE_ROWS = 128
HALO = 16
MOE_TILE = 512
DMA_UNROLL = 8


def _params(sem, vmem=V7X_VMEM_LIMIT):
    return pltpu.CompilerParams(dimension_semantics=sem, vmem_limit_bytes=vmem)


def _sds(shape, dtype):
    return jax.ShapeDtypeStruct(shape, dtype)


def _tile(n, pref):
    t = min(pref, n)
    while n % t:
        t //= 2
    return t


def _modulate(x, g, shift, scale):
    ms = jnp.mean(x * x, axis=-1, keepdims=True)
    y = x * lax.rsqrt(ms + EPS) * g
    return y * (1.0 + scale) + shift


def _mod_kernel(c_ref, w_ref, b_ref, o_ref):
    c = c_ref[...]
    s = (c * jax.nn.sigmoid(c)).astype(BF16)
    o_ref[...] = jnp.dot(s, w_ref[...].astype(BF16), preferred_element_type=F32) + b_ref[...]


def _mod_all(c_all, ada_w, ada_b):
    depth, d, n = ada_w.shape
    tn = _tile(n, 1024)
    return pl.pallas_call(
        _mod_kernel,
        grid=(depth, n // tn),
        in_specs=[pl.BlockSpec((MOD_ROWS, d), lambda l, j: (0, 0)),
                  pl.BlockSpec((None, d, tn), lambda l, j: (l, 0, j)),
                  pl.BlockSpec((None, 1, tn), lambda l, j: (l, 0, j))],
        out_specs=pl.BlockSpec((None, MOD_ROWS, tn), lambda l, j: (l, 0, j)),
        out_shape=_sds((depth, MOD_ROWS, n), F32),
        compiler_params=_params(("parallel", "parallel")),
        name="adaln_mod",
    )(c_all, ada_w, ada_b.reshape(depth, 1, n))


def _win_kernel(fuse_moe, *refs):
    if fuse_moe:
        x_ref, moe_ref, g2_ref, sh_ref, sc_ref, g_ref, w_ref, o_ref, h_ref = refs
    else:
        x_ref, sh_ref, sc_ref, g_ref, w_ref, o_ref, h_ref = refs

    @pl.when(pl.program_id(1) == 0)
    def _():
        def chunk(c, carry):
            rs = pl.ds(pl.multiple_of(c * PROLOGUE_ROWS, PROLOGUE_ROWS), PROLOGUE_ROWS)
            x = x_ref[rs, :]
            if fuse_moe:
                x = x + g2_ref[...] * moe_ref[rs, :]
            h_ref[rs, :] = _modulate(x, g_ref[...], sh_ref[...], sc_ref[...]).astype(BF16)
            return carry
        lax.fori_loop(0, x_ref.shape[0] // PROLOGUE_ROWS, chunk, 0)

    o_ref[...] = jnp.dot(h_ref[...], w_ref[...], preferred_element_type=F32).astype(o_ref.dtype)


def _win_call(x, moe, mod3, mod3_prev, norm_g, w_bf, rows, seq_len, n_batch):
    d = x.shape[1]
    n = w_bf.shape[1]
    tm = _tile(math.gcd(seq_len, rows), 1024)
    tn = d // 2
    fuse = moe is not None

    def mrow(i):
        return jnp.minimum((i * tm) // seq_len, n_batch)

    x_spec = pl.BlockSpec((tm, d), lambda i, j: (i, 0))

    def mod_spec(col):
        return pl.BlockSpec((None, 1, d), lambda i, j: (mrow(i), 0, col))

    in_specs = [x_spec]
    args = [x]
    if fuse:
        in_specs += [pl.BlockSpec((tm, d), lambda i, j: (i, 0), pipeline_mode=pl.Buffered(1)), mod_spec(5)]
        args += [moe, mod3_prev]
    in_specs += [mod_spec(0), mod_spec(1), pl.BlockSpec((1, d), lambda i, j: (0, 0)),
                 pl.BlockSpec((d, tn), lambda i, j: (0, j))]
    args += [mod3, mod3, norm_g.reshape(1, d), w_bf]
    return pl.pallas_call(
        functools.partial(_win_kernel, fuse),
        grid=(rows // tm, n // tn),
        in_specs=in_specs,
        out_specs=pl.BlockSpec((tm, tn), lambda i, j: (i, j)),
        out_shape=_sds((x.shape[0], n), BF16),
        scratch_shapes=[pltpu.VMEM((tm, d), BF16)],
        compiler_params=_params(("parallel", "arbitrary")),
        name="in_proj",
    )(*args)


def _lru_tile(d, j, nt_c, nt_l):
    is_ctx = j < nt_c
    nt = jnp.where(is_ctx, nt_c, nt_l)
    jl = jnp.where(is_ctx, j, j - nt_c)
    return is_ctx, jl + d * (nt - 1 - 2 * jl), nt


def _lru_kernel(tl, nt_c, nt_l, heads, xm_ref, xp_ref, xn_ref, cw_ref, cb_ref, wg_ref, bg_ref, lam_ref,
                o_ref, pad_ref, a_ref, b_ref, hs_ref, hc_ref):
    d = pl.program_id(0)
    j = pl.program_id(2)
    _, jt, nt = _lru_tile(d, j, nt_c, nt_l)
    w = a_ref.shape[1]
    bw = w // heads

    pad_ref[8:8 + tl, :] = xm_ref[...].astype(F32)
    prev = xp_ref[...].astype(F32)[HALO - 8:HALO]
    nxt = xn_ref[...].astype(F32)[0:8]
    pad_ref[0:8, :] = jnp.where(jt > 0, prev, 0.0)
    pad_ref[8 + tl:16 + tl, :] = jnp.where(jt < nt - 1, nxt, 0.0)
    cw = cw_ref[...]
    u = (cw[0:1] * pad_ref[7:7 + tl, :] + cw[1:2] * pad_ref[8:8 + tl, :]
         + cw[2:3] * pad_ref[9:9 + tl, :] + cw[3:4] * pad_ref[10:10 + tl, :]) + cb_ref[...]
    ub = u.astype(BF16)
    lam = lam_ref[...]
    sp = jnp.maximum(-lam, 0.0) + jnp.log1p(jnp.exp(-jnp.abs(lam)))
    for hd in range(heads):
        cs = slice(hd * bw, (hd + 1) * bw)
        g = jnp.dot(ub[:, cs], wg_ref[hd], preferred_element_type=F32) + bg_ref[hd]
        r = jax.nn.sigmoid(g[:, :bw])
        i = jax.nn.sigmoid(g[:, bw:])
        log_a = (-LRU_C) * r * sp[:, cs]
        a = jnp.exp(log_a)
        a_ref[:, cs] = a
        b_ref[:, cs] = jnp.sqrt(1.0 - a * a) * (i * u[:, cs])

    @pl.when(j == 0)
    def _():
        hc_ref[...] = jnp.zeros_like(hc_ref)

    def step(s, h):
        t = s + d * (tl - 1 - 2 * s)
        h = a_ref[pl.ds(t, 1), :] * h + b_ref[pl.ds(t, 1), :]
        hs_ref[pl.ds(t, 1), :] = h
        return h

    h = lax.fori_loop(0, tl, step, hc_ref[...], unroll=8)
    hc_ref[...] = h
    o_ref[...] = hs_ref[...].astype(o_ref.dtype)


def _lru_call(z, lw, n_lat, n_ctx, n_batch):
    w = lw["conv_w"].shape[1]
    heads = lw["wg"].shape[1]
    tl = _tile(math.gcd(n_lat, n_ctx), 512)
    nt_c, nt_l = n_ctx // tl, n_lat // tl
    ctx_blk0 = n_batch * nt_l
    per = tl // HALO
    last_h = n_batch * (n_lat + n_ctx) // HALO - 1

    def blk(d, b, j):
        is_ctx, jt, _ = _lru_tile(d, j, nt_c, nt_l)
        return jnp.where(is_ctx, ctx_blk0 + b * nt_c + jt, b * nt_l + jt)

    in_specs = [pl.BlockSpec((tl, w), lambda d, b, j: (blk(d, b, j), 0)),
                pl.BlockSpec((HALO, w), lambda d, b, j: (jnp.maximum(blk(d, b, j) * per - 1, 0), 0)),
                pl.BlockSpec((HALO, w), lambda d, b, j: (jnp.minimum((blk(d, b, j) + 1) * per, last_h), 0)),
                pl.BlockSpec((4, w), lambda d, b, j: (0, 0)),
                pl.BlockSpec((1, w), lambda d, b, j: (0, 0)),
                pl.BlockSpec((None, heads, w // heads, 2 * (w // heads)), lambda d, b, j: (d, 0, 0, 0)),
                pl.BlockSpec((None, heads, 1, 2 * (w // heads)), lambda d, b, j: (d, 0, 0, 0)),
                pl.BlockSpec((None, 1, w), lambda d, b, j: (d, 0, 0))]
    return pl.pallas_call(
        functools.partial(_lru_kernel, tl, nt_c, nt_l, heads),
        grid=(2, n_batch, nt_c + nt_l),
        in_specs=in_specs,
        out_specs=pl.BlockSpec((None, tl, w), lambda d, b, j: (d, blk(d, b, j), 0)),
        out_shape=_sds((2, n_batch * (n_lat + n_ctx), w), BF16),
        scratch_shapes=[pltpu.VMEM((tl + 16, w), F32), pltpu.VMEM((tl, w), F32),
                        pltpu.VMEM((tl, w), F32), pltpu.VMEM((tl, w), F32), pltpu.VMEM((1, w), F32)],
        compiler_params=_params(("arbitrary", "arbitrary", "arbitrary")),
        name="rglru_scan",
    )(z, z, z, lw["conv_w"], lw["conv_b"], lw["wg"], lw["bg"], lw["lam"])


def _sgu_kernel(groups, u_ref, v_ref, g_ref, ws_ref, bias_ref, o_ref):
    v = v_ref[...].astype(F32)
    mu = jnp.mean(v, axis=-1, keepdims=True)
    vc = v - mu
    var = jnp.mean(vc * vc, axis=-1, keepdims=True)
    vn = (vc * lax.rsqrt(var + EPS) * g_ref[...]).astype(BF16)
    ts, w = v.shape
    gw = w // groups
    for ck in range(ts // SGU_CHUNK):
        rs = slice(ck * SGU_CHUNK, (ck + 1) * SGU_CHUNK)
        for g in range(groups):
            cs = slice(g * gw, (g + 1) * gw)
            mixed = jnp.dot(ws_ref[g], vn[rs, cs], preferred_element_type=F32) + bias_ref[:, cs]
            o_ref[rs, cs] = (u_ref[rs, cs].astype(F32) * mixed).astype(o_ref.dtype)


def _sgu_call(z, norm_g, ws_bf, bias_full, rows):
    groups = ws_bf.shape[0]
    w = norm_g.shape[0]
    ts = _tile(rows, 512)
    return pl.pallas_call(
        functools.partial(_sgu_kernel, groups),
        grid=(rows // ts,),
        in_specs=[pl.BlockSpec((ts, w), lambda i: (i, 2)),
                  pl.BlockSpec((ts, w), lambda i: (i, 3)),
                  pl.BlockSpec((1, w), lambda i: (0, 0)),
                  pl.BlockSpec((groups, SGU_CHUNK, SGU_CHUNK), lambda i: (0, 0, 0)),
                  pl.BlockSpec((SGU_CHUNK, w), lambda i: (0, 0))],
        out_specs=pl.BlockSpec((ts, w), lambda i: (i, 0)),
        out_shape=_sds((rows, w), BF16),
        compiler_params=_params(("parallel",)),
        name="sgu",
    )(z, z, norm_g.reshape(1, w), ws_bf, bias_full)


def _hconv_kernel(tl, nt, xm_ref, xp_ref, xn_ref, cw_ref, cb_ref, o_ref, pad_ref):
    j = pl.program_id(2)
    pad_ref[8:8 + tl, :] = xm_ref[...].astype(F32)
    prev = xp_ref[...].astype(F32)[HALO - 8:HALO]
    nxt = xn_ref[...].astype(F32)[0:8]
    pad_ref[0:8, :] = jnp.where(j > 0, prev, 0.0)
    pad_ref[8 + tl:16 + tl, :] = jnp.where(j < nt - 1, nxt, 0.0)
    cw = cw_ref[...]
    u = (cw[0:1] * pad_ref[7:7 + tl, :] + cw[1:2] * pad_ref[8:8 + tl, :]
         + cw[2:3] * pad_ref[9:9 + tl, :]) + cb_ref[...]
    o_ref[...] = u.astype(o_ref.dtype)


def _hconv_call(z, conv_w, conv_b, row0, seq_len, n_batch):
    w = conv_w.shape[1] // 3
    tl = _tile(seq_len, 512)
    nt = seq_len // tl
    blk0 = row0 // tl
    hb0 = row0 // HALO
    per = tl // HALO
    last_h = (row0 + n_batch * seq_len) // HALO - 1
    return pl.pallas_call(
        functools.partial(_hconv_kernel, tl, nt),
        grid=(3, n_batch, nt),
        in_specs=[pl.BlockSpec((tl, w), lambda s, b, j: (blk0 + b * nt + j, 4 + s)),
                  pl.BlockSpec((HALO, w), lambda s, b, j: (jnp.maximum(hb0 + (b * nt + j) * per - 1, 0), 4 + s)),
                  pl.BlockSpec((HALO, w), lambda s, b, j: (jnp.minimum(hb0 + (b * nt + j + 1) * per, last_h), 4 + s)),
                  pl.BlockSpec((3, w), lambda s, b, j: (0, s)),
                  pl.BlockSpec((1, w), lambda s, b, j: (0, s))],
        out_specs=pl.BlockSpec((None, tl, w), lambda s, b, j: (s, b * nt + j, 0)),
        out_shape=_sds((3, n_batch * seq_len, w), BF16),
        scratch_shapes=[pltpu.VMEM((tl + 16, w), F32)],
        compiler_params=_params(("parallel", "parallel", "arbitrary")),
        name="hyena_conv",
    )(z, z, z, conv_w, conv_b.reshape(1, 3 * w))


def _hfilt_kernel(feat_ref, t_ref, dl_ref, w1_ref, b1_ref, w2_ref, b2_ref, w3_ref, fr_ref, o_ref):
    hi = lax.Precision.HIGHEST
    fr = fr_ref[...]
    h = jnp.sin(fr * (jnp.dot(feat_ref[...], w1_ref[...], precision=hi, preferred_element_type=F32) + b1_ref[...]))
    h = jnp.sin(fr * (jnp.dot(h, w2_ref[...], precision=hi, preferred_element_type=F32) + b2_ref[...]))
    h = jnp.dot(h, w3_ref[...], precision=hi, preferred_element_type=F32)
    w = h.shape[1] // 2
    h = h * jnp.exp(-t_ref[...] * dl_ref[...])
    fwd = h[:, :w]
    row = lax.broadcasted_iota(I32, (h.shape[0], 1), 0) + pl.program_id(1) * h.shape[0]
    bwd = jnp.where(row > 0, h[:, w:], 0.0)
    o_ref[:, :w] = (fwd + bwd).astype(o_ref.dtype)
    o_ref[:, w:] = (fwd - bwd).astype(o_ref.dtype)


def _hfilt_call(consts, hw):
    feat, tcol, deltas = consts["feat"], consts["t"], consts["deltas"]
    seq_len, fpad = feat.shape
    hid = hw["w2"].shape[0]
    w = deltas.shape[1]
    orders = hw["w3"].shape[1] // (2 * w)
    tr = _tile(seq_len, 512)
    return pl.pallas_call(
        _hfilt_kernel,
        grid=(orders, seq_len // tr),
        in_specs=[pl.BlockSpec((tr, fpad), lambda o, i: (i, 0)),
                  pl.BlockSpec((tr, 1), lambda o, i: (i, 0)),
                  pl.BlockSpec((1, 2 * w), lambda o, i: (0, 0)),
                  pl.BlockSpec((fpad, hid), lambda o, i: (0, 0)),
                  pl.BlockSpec((1, hid), lambda o, i: (0, 0)),
                  pl.BlockSpec((hid, hid), lambda o, i: (0, 0)),
                  pl.BlockSpec((1, hid), lambda o, i: (0, 0)),
                  pl.BlockSpec((hid, 2 * w), lambda o, i: (0, o)),
                  pl.BlockSpec((1, hid), lambda o, i: (0, 0))],
        out_specs=pl.BlockSpec((None, tr, 2 * w), lambda o, i: (o, i, 0)),
        out_shape=_sds((orders, seq_len, 2 * w), BF16),
        compiler_params=_params(("parallel", "parallel")),
        name="hyena_filter",
    )(feat, tcol, jnp.concatenate([deltas, deltas], axis=1), hw["w1"], hw["b1"], hw["w2"], hw["b2"],
      hw["w3"], hw["freq"])


_R8 = math.sqrt(0.5)
_FWD_COEF = np.array([[1, 1, 1, 1, 1, -1, 1, -1],
                      [1, _R8, 0, -_R8, 0, -_R8, -1, -_R8],
                      [1, 0, -1, 0, 0, -1, 0, 1],
                      [1, -_R8, 0, _R8, 0, -_R8, 1, -_R8]],
                     np.float32)
N_FAM = 4


def _radix8_inputs(coef_ref, g, x_ref, a_ref, n2, rows):
    def chunk(c, carry):
        r0 = pl.multiple_of(c * rows, rows)
        xs = [x_ref[pl.ds(t1 * n2 + r0, rows), :].astype(F32) for t1 in range(4)]
        for half in range(2):
            acc = coef_ref[g, 4 * half] * xs[0]
            for t1 in range(1, 4):
                acc = acc + coef_ref[g, 4 * half + t1] * xs[t1]
            a_ref[pl.ds(half * n2 + r0, rows), :] = acc.astype(a_ref.dtype)
        return carry
    lax.fori_loop(0, n2 // rows, chunk, 0)


def _hspec_kernel(tk, n2, rows, coef_ref, d_ref, p_ref, q_ref, ma_ref, mb_ref, ap_ref, aq_ref):
    g = pl.program_id(0)
    m = pl.program_id(3)

    @pl.when(m == 0)
    def _():
        _radix8_inputs(coef_ref, g, p_ref, ap_ref, n2, rows)
        _radix8_inputs(coef_ref, g, q_ref, aq_ref, n2, rows)

    re = jnp.dot(d_ref[:tk, :], ap_ref[...], preferred_element_type=F32)
    kfi = jnp.dot(d_ref[tk:, :], aq_ref[...], preferred_element_type=F32)
    first = (g == 0) & (m == 0)
    kfi = jnp.where((lax.broadcasted_iota(I32, (tk, 1), 0) == 0) & first, 0.0, kfi)
    ma_ref[:tk, :] = re
    ma_ref[tk:, :] = re
    mb_ref[:tk, :] = -kfi
    mb_ref[tk:, :] = kfi

    @pl.when(first)
    def _():
        nyq = jnp.dot(d_ref[tk:tk + HALO, :], ap_ref[...], preferred_element_type=F32)
        row0 = lax.broadcasted_iota(I32, (HALO, 1), 0) == 0
        ma_ref[tk:tk + HALO, :] = jnp.where(row0, nyq, re[:HALO])


def _hspec_call(consts, pq):
    orders, seq_len, w2 = pq.shape
    w = w2 // 2
    n2, tk = consts["n2"], consts["tk"]
    tn = _tile(w, 512)
    nn = w // tn
    rows = _tile(n2, 256)
    grid_spec = pltpu.PrefetchScalarGridSpec(
        num_scalar_prefetch=0,
        grid=(N_FAM, orders, nn, 2),
        in_specs=[pl.BlockSpec(memory_space=pltpu.SMEM),
                  pl.BlockSpec((None, 2 * tk, 2 * n2), lambda g, o, n, m: (g, m, 0)),
                  pl.BlockSpec((None, seq_len, tn), lambda g, o, n, m: (o, 0, n)),
                  pl.BlockSpec((None, seq_len, tn), lambda g, o, n, m: (o, 0, nn + n))],
        out_specs=[pl.BlockSpec((None, 2 * tk, tn), lambda g, o, n, m: (o, 2 * g + m, n)),
                   pl.BlockSpec((None, 2 * tk, tn), lambda g, o, n, m: (o, 2 * g + m, n))],
        scratch_shapes=[pltpu.VMEM((2 * n2, tn), BF16), pltpu.VMEM((2 * n2, tn), BF16)])
    return pl.pallas_call(
        functools.partial(_hspec_kernel, tk, n2, rows),
        grid_spec=grid_spec,
        out_shape=[_sds((orders, 2 * seq_len, w), F32), _sds((orders, 2 * seq_len, w), F32)],
        compiler_params=_params(("parallel", "parallel", "parallel", "arbitrary")),
        name="hyena_filter_spectrum",
    )(consts["coef"], consts["dfwd"], pq, pq)


def _hfwd_kernel(tk, n2, rows, coef_ref, d_ref, x_ref, ma_ref, mb_ref, o_ref, a_ref):
    g = pl.program_id(0)

    @pl.when(pl.program_id(3) == 0)
    def _():
        _radix8_inputs(coef_ref, g, x_ref, a_ref, n2, rows)

    s = jnp.dot(d_ref[...], a_ref[...], preferred_element_type=F32)
    sw = jnp.concatenate([s[tk:], s[:tk]], axis=0)
    o_ref[...] = (s * ma_ref[...] + sw * mb_ref[...]).astype(o_ref.dtype)


def _hfwd_call(consts, x2d, xblk0, ma, mb, order, n_batch):
    n2, tk = consts["n2"], consts["tk"]
    seq_len = 4 * n2
    w = x2d.shape[1]
    tn = _tile(w, 512)
    rows = _tile(n2, 256)
    grid_spec = pltpu.PrefetchScalarGridSpec(
        num_scalar_prefetch=0,
        grid=(N_FAM, n_batch, w // tn, 2),
        in_specs=[pl.BlockSpec(memory_space=pltpu.SMEM),
                  pl.BlockSpec((None, 2 * tk, 2 * n2), lambda g, b, n, m: (g, m, 0)),
                  pl.BlockSpec((seq_len, tn), lambda g, b, n, m: (xblk0 + b, n)),
                  pl.BlockSpec((None, 2 * tk, tn), lambda g, b, n, m: (order, 2 * g + m, n)),
                  pl.BlockSpec((None, 2 * tk, tn), lambda g, b, n, m: (order, 2 * g + m, n))],
        out_specs=pl.BlockSpec((None, 2 * tk, tn), lambda g, b, n, m: (b, 2 * g + m, n)),
        scratch_shapes=[pltpu.VMEM((2 * n2, tn), BF16)])
    return pl.pallas_call(
        functools.partial(_hfwd_kernel, tk, n2, rows),
        grid_spec=grid_spec,
        out_shape=_sds((n_batch, 2 * seq_len, w), BF16),
        compiler_params=_params(("parallel", "parallel", "parallel", "arbitrary")),
        name="hyena_dft",
    )(consts["coef"], consts["dfwd"], x2d, ma, mb)


def _hinv_kernel(n2, tt, e_ref, p_ref, v_ref, x_ref, sk_ref, o_ref):
    top, bot = [], []
    for g in range(N_FAM):
        r = jnp.dot(e_ref[g], p_ref[g * 2 * n2:(g + 1) * 2 * n2, :], preferred_element_type=F32)
        top.append(r[:tt])
        bot.append(r[tt:])
    ys = (top[0] + bot[0] + top[1] + top[2] + top[3],
          top[0] - bot[0] + _R8 * (top[1] - bot[1]) - bot[2] - _R8 * (top[3] + bot[3]),
          top[0] + bot[0] - bot[1] - top[2] + bot[3],
          top[0] - bot[0] - _R8 * (top[1] + bot[1]) + bot[2] + _R8 * (top[3] - bot[3]))
    for t1 in range(4):
        v = v_ref[t1].astype(F32)
        o_ref[t1] = (x_ref[t1].astype(F32) * (ys[t1] + v * sk_ref[...])).astype(o_ref.dtype)


def _hinv_call(consts, p, v4, vblk0, x4, xblk0, skip3, order, n_batch):
    n2, tt = consts["n2"], consts["tt"]
    w = v4.shape[3]
    tn = _tile(w, 512)
    blk = lambda off: pl.BlockSpec((None, 4, tt, tn), lambda j, b, n: (off + b, 0, j, n))
    return pl.pallas_call(
        functools.partial(_hinv_kernel, n2, tt),
        grid=(n2 // tt, n_batch, w // tn),
        in_specs=[pl.BlockSpec((N_FAM, 2 * tt, 2 * n2), lambda j, b, n: (0, j, 0)),
                  pl.BlockSpec((None, 8 * n2, tn), lambda j, b, n: (b, 0, n)),
                  blk(vblk0), blk(xblk0),
                  pl.BlockSpec((None, 1, tn), lambda j, b, n: (order, 0, n))],
        out_specs=blk(0),
        out_shape=_sds((n_batch, 4, n2, w), BF16),
        compiler_params=_params(("parallel", "arbitrary", "arbitrary")),
        name="hyena_idft",
    )(consts["einv"], p, v4, x4, skip3)


def _dft_consts(seq_len, w, emb):
    f32 = F32
    n2 = seq_len // 4
    tk = n2 // 2
    tt = _tile(n2, 256)
    nfull = 2 * seq_len
    t2 = jnp.arange(n2, dtype=I32)

    def cs_table(rows):
        ang = ((rows[:, None] * t2[None, :]) % nfull).astype(f32) * (math.pi / seq_len)
        return jnp.cos(ang), jnp.sin(ang)

    cb, sb = cs_table(8 * t2)
    co, so = cs_table(jnp.array([0, 4, 1, 2, 3], I32))
    c = co[:, None, :] * cb[None, :, :] - so[:, None, :] * sb[None, :, :]
    s = so[:, None, :] * cb[None, :, :] + co[:, None, :] * sb[None, :, :]
    nyq = jnp.where(t2 % 2 == 0, 1.0, -1.0).astype(f32)
    zero = jnp.zeros((tk, n2), f32)

    def tiled_rows(re, im):
        nb = re.shape[0] // tk
        return jnp.stack([re.reshape(nb, tk, -1), im.reshape(nb, tk, -1)], axis=1).reshape(2 * re.shape[0], -1)

    c0, s0, c4, s4 = c[0, :tk], s[0, :tk], c[1, :tk], s[1, :tk]
    dfwd = [jnp.concatenate([jnp.concatenate([c0, zero], axis=1),
                             jnp.concatenate([(-s0).at[0].set(nyq), zero], axis=1),
                             jnp.concatenate([zero, c4], axis=1),
                             jnp.concatenate([zero, -s4], axis=1)], axis=0)]
    for g in range(1, N_FAM):
        cg, sg = c[g + 1], s[g + 1]
        dfwd.append(tiled_rows(jnp.concatenate([cg, sg], axis=1), jnp.concatenate([-sg, cg], axis=1)))
    dfwd = jnp.stack(dfwd).astype(BF16)

    def tiled_cols(re, im):
        nb = re.shape[1] // tk
        return jnp.stack([re.reshape(n2, nb, tk), im.reshape(n2, nb, tk)], axis=2).reshape(n2, 2 * re.shape[1])

    sc = 2.0 / nfull
    zcol = jnp.zeros((n2, 2 * tk), f32)
    top0 = jnp.concatenate([(c0.T * sc).at[:, 0].multiply(0.5), (-s0.T * sc).at[:, 0].set(nyq / nfull), zcol], axis=1)
    bot0 = jnp.concatenate([zcol, c4.T * sc, -s4.T * sc], axis=1)
    tops, bots = [top0], [bot0]
    for g in range(1, N_FAM):
        cg, sg = c[g + 1].T * sc, s[g + 1].T * sc
        tops.append(tiled_cols(cg, -sg))
        bots.append(tiled_cols(sg, cg))
    einv = jnp.stack([jnp.stack([tp.reshape(n2 // tt, tt, 2 * n2), bt.reshape(n2 // tt, tt, 2 * n2)], axis=1)
                      .reshape(2 * n2, 2 * n2) for tp, bt in zip(tops, bots)]).astype(BF16)
    bands = (emb - 1) // 2
    t = jnp.linspace(0.0, 1.0, seq_len, dtype=f32)[:, None]
    wv = 2.0 * math.pi * jnp.arange(seq_len, dtype=f32)[:, None] / seq_len
    fb = jnp.linspace(1e-4, bands - 1, bands, dtype=f32)[None, :]
    feat = jnp.concatenate([t, jnp.cos(fb * wv), -jnp.sin(fb * wv)], axis=-1)
    fpad = 128
    feat = jnp.pad(feat, ((0, 0), (0, fpad - emb)))
    deltas = jnp.abs(jnp.linspace(HY_MIN_DECAY, HY_MAX_DECAY, w, dtype=f32))[None, :]
    return {"dfwd": dfwd, "einv": einv, "coef": jnp.asarray(_FWD_COEF), "feat": feat, "t": t, "deltas": deltas,
            "n2": n2, "tk": tk, "tt": tt}


def _hyena_call(z, consts, hw, row0, seq_len, n_batch):
    n2 = consts["n2"]
    w = hw["skip3"].shape[2]
    hc = _hconv_call(z, hw["conv_w"], hw["conv_b"], row0, seq_len, n_batch)
    hc2 = hc.reshape(3 * n_batch * seq_len, w)
    hc4 = hc.reshape(3 * n_batch, 4, n2, w)
    pq = _hfilt_call(consts, hw)
    ma, mb = _hspec_call(consts, pq)
    p0 = _hfwd_call(consts, hc2, 0, ma, mb, 0, n_batch)
    s1 = _hinv_call(consts, p0, hc4, 0, hc4, n_batch, hw["skip3"], 0, n_batch)
    p1 = _hfwd_call(consts, s1.reshape(n_batch * seq_len, w), 0, ma, mb, 1, n_batch)
    yc = _hinv_call(consts, p1, s1, 0, hc4, 2 * n_batch, hw["skip3"], 1, n_batch)
    return yc.reshape(n_batch * seq_len, w)


def _merge_kernel(lat_tiles, fuse_moe, hf_ref, hb_ref, ga_ref, yb_ref, ycl_ref, ycc_ref, g0a, g0b, g1a, g1b, g2a, g2b,
                  x_ref, gate_ref, wbr_ref, wout_ref, *rest):
    if fuse_moe:
        moe_ref, mg_ref, o_ref, m_ref = rest
    else:
        o_ref, m_ref = rest
    ga = ga_ref[...].astype(F32)
    ya = ((hf_ref[...].astype(F32) + hb_ref[...].astype(F32)) * jax.nn.gelu(ga)).astype(BF16)
    yc = jnp.where(pl.program_id(0) < lat_tiles, ycl_ref[...], ycc_ref[...])
    ys = (ya, yb_ref[...], yc)
    w = ya.shape[1]
    gates = ((g0a, g0b), (g1a, g1b), (g2a, g2b))
    for half in range(2):
        cs = slice(half * w, (half + 1) * w)
        acc = None
        for br in range(3):
            gt = jax.nn.sigmoid(gates[br][half][...].astype(F32))
            t = gt * jnp.dot(ys[br], wbr_ref[br, :, cs], preferred_element_type=F32)
            acc = t if acc is None else acc + t
        m_ref[:, cs] = acc.astype(BF16)
    out = jnp.dot(m_ref[...], wout_ref[...], preferred_element_type=F32)
    x = x_ref[...]
    if fuse_moe:
        x = x + mg_ref[...] * moe_ref[...]
    o_ref[...] = x + gate_ref[...] * out


def _merge_call(hlru, z, yb, yc_lat, yc_ctx, x, moe, mod3, mod3_prev, wbr_bf, wout_bf, rows, seq_len, n_batch):
    d = x.shape[1]
    fuse = moe is not None
    w = d // 2
    tm = _tile(math.gcd(seq_len, rows), 256)
    lat_tiles = yc_lat.shape[0] // tm
    ctx_last = yc_ctx.shape[0] // tm - 1

    def mrow(i):
        return jnp.minimum((i * tm) // seq_len, n_batch)

    def zcol(c):
        return pl.BlockSpec((tm, w), lambda i: (i, c))

    const = dict(pipeline_mode=pl.Buffered(1))
    in_specs = [pl.BlockSpec((None, tm, w), lambda i: (0, i, 0)),
                pl.BlockSpec((None, tm, w), lambda i: (1, i, 0)),
                zcol(1), zcol(0),
                pl.BlockSpec((tm, w), lambda i: (jnp.minimum(i, lat_tiles - 1), 0)),
                pl.BlockSpec((tm, w), lambda i: (jnp.clip(i - lat_tiles, 0, ctx_last), 0))] + [
                zcol(7 + c) for c in range(6)] + [
                pl.BlockSpec((tm, d), lambda i: (i, 0)),
                pl.BlockSpec((None, 1, d), lambda i: (mrow(i), 0, 2)),
                pl.BlockSpec((3, w, d), lambda i: (0, 0, 0), **const),
                pl.BlockSpec((d, d), lambda i: (0, 0), **const)]
    args = [hlru, hlru, z, yb, yc_lat, yc_ctx, z, z, z, z, z, z, x, mod3, wbr_bf, wout_bf]
    if fuse:
        in_specs += [pl.BlockSpec((tm, d), lambda i: (i, 0)),
                     pl.BlockSpec((None, 1, d), lambda i: (mrow(i), 0, 5))]
        args += [moe, mod3_prev]
    return pl.pallas_call(
        functools.partial(_merge_kernel, lat_tiles, fuse),
        grid=(rows // tm,),
        in_specs=in_specs,
        out_specs=pl.BlockSpec((tm, d), lambda i: (i, 0)),
        out_shape=_sds((rows, d), F32),
        scratch_shapes=[pltpu.VMEM((tm, d), BF16)],
        compiler_params=_params(("parallel",)),
        name="merge_out_proj",
    )(*args)


def _router_kernel(x_ref, sh_ref, sc_ref, g_ref, rw_ref, rb_ref, h_ref, cls_ref, wt_ref):
    h = _modulate(x_ref[...], g_ref[...], sh_ref[...], sc_ref[...])
    h_ref[...] = h
    logits = lax.dot_general(rw_ref[...], h, (((1,), (1,)), ((), ())),
                             precision=lax.Precision.HIGHEST, preferred_element_type=F32)
    s = jax.nn.sigmoid(logits)
    sb = s + rb_ref[...]
    rows_b = [sb[e:e + 1, :] for e in range(N_GROUPS * EXP_PER_GROUP)]
    rows_s = [s[e:e + 1, :] for e in range(N_GROUPS * EXP_PER_GROUP)]
    gscore = []
    for g in range(N_GROUPS):
        v = rows_b[EXP_PER_GROUP * g:EXP_PER_GROUP * (g + 1)]
        best = None
        for a in range(EXP_PER_GROUP):
            for b in range(a + 1, EXP_PER_GROUP):
                pr = v[a] + v[b]
                best = pr if best is None else jnp.maximum(best, pr)
        gscore.append(best)
    best = gscore[0]
    gsel = jnp.zeros(best.shape, I32)
    for g in range(1, N_GROUPS):
        better = gscore[g] > best
        best = jnp.where(better, gscore[g], best)
        gsel = jnp.where(better, g, gsel)

    def pick(rows, i):
        out = rows[i]
        for g in range(1, N_GROUPS):
            out = jnp.where(gsel == g, rows[EXP_PER_GROUP * g + i], out)
        return out

    vb = [pick(rows_b, i) for i in range(EXP_PER_GROUP)]
    vs = [pick(rows_s, i) for i in range(EXP_PER_GROUP)]
    sel = []
    for i in range(EXP_PER_GROUP):
        rank = jnp.zeros(best.shape, I32)
        for jx in range(EXP_PER_GROUP):
            if jx == i:
                continue
            ahead = (vb[jx] > vb[i]) if jx > i else (vb[jx] >= vb[i])
            rank = rank + ahead.astype(I32)
        sel.append(rank < 2)
    j_hi = jnp.where(sel[3], 3, jnp.where(sel[2], 2, 1))
    w_lo = jnp.where(sel[0], vs[0], jnp.where(sel[1], vs[1], vs[2]))
    w_hi = jnp.where(sel[3], vs[3], jnp.where(sel[2], vs[2], vs[1]))
    den = w_lo + w_hi
    pair = jnp.where(sel[0], j_hi - 1, jnp.where(sel[1], j_hi + 1, 5))
    cls_ref[...] = gsel * N_PAIRS + pair
    wt_ref[0:1, :] = w_lo / den
    wt_ref[1:2, :] = w_hi / den


def _router_call(x, mod3, norm_g, rw_t, rb, rows, seq_len, n_batch):
    d = x.shape[1]
    e = rw_t.shape[0]
    tm = _tile(math.gcd(seq_len, rows), 512)

    def mrow(i):
        return jnp.minimum((i * tm) // seq_len, n_batch)

    return pl.pallas_call(
        _router_kernel,
        grid=(rows // tm,),
        in_specs=[pl.BlockSpec((tm, d), lambda i: (i, 0)),
                  pl.BlockSpec((None, 1, d), lambda i: (mrow(i), 0, 3)),
                  pl.BlockSpec((None, 1, d), lambda i: (mrow(i), 0, 4)),
                  pl.BlockSpec((1, d), lambda i: (0, 0)),
                  pl.BlockSpec((e, d), lambda i: (0, 0)),
                  pl.BlockSpec((e, 1), lambda i: (0, 0))],
        out_specs=[pl.BlockSpec((tm, d), lambda i: (i, 0)),
                   pl.BlockSpec((1, tm), lambda i: (0, i)),
                   pl.BlockSpec((2, tm), lambda i: (0, i))],
        out_shape=[_sds((rows, d), F32), _sds((1, rows), I32), _sds((2, rows), F32)],
        compiler_params=_params(("parallel",)),
        name="moe_router",
    )(x, mod3, mod3, norm_g.reshape(1, d), rw_t, rb.reshape(e, 1))


_PAIR_LO = np.array([0, 0, 0, 1, 1, 2], np.int32)
_PAIR_HI = np.array([1, 2, 3, 2, 3, 3], np.int32)
_CLS_LO = np.concatenate([EXP_PER_GROUP * g + _PAIR_LO for g in range(N_GROUPS)])
_CLS_HI = np.concatenate([EXP_PER_GROUP * g + _PAIR_HI for g in range(N_GROUPS)])


def _moe_plan(cls, wts, tm):
    t = cls.shape[0]
    n_tiles = t // tm + N_CLASSES
    r = n_tiles * tm
    cnt = jnp.sum((cls[:, None] == jnp.arange(N_CLASSES, dtype=I32)[None, :]).astype(I32), axis=0)
    nt_c = (cnt + tm - 1) // tm
    t_end = jnp.cumsum(nt_c)
    t_off = t_end - nt_c
    tile = jnp.arange(n_tiles, dtype=I32)
    total = t_end[-1]
    valid = (tile < total).astype(I32)
    tcls = jnp.sum((t_end[None, :] <= jnp.minimum(tile, total - 1)[:, None]).astype(I32), axis=1)
    pad_end = jnp.cumsum(nt_c * tm - cnt)
    pcls = jnp.sum((pad_end[None, :] <= jnp.arange(r - t, dtype=I32)[:, None]).astype(I32), axis=1)
    keys = jnp.concatenate([2 * cls, 2 * pcls + 1])
    tok = jnp.concatenate([jnp.arange(t, dtype=I32), jnp.full((r - t,), -1, I32)])
    wpad = jnp.zeros((r - t,), F32)
    _, dst, w0, w1 = lax.sort((keys, tok, jnp.concatenate([wts[0], wpad]), jnp.concatenate([wts[1], wpad])),
                              num_keys=1, is_stable=True)
    src = jnp.maximum(dst, 0)
    wsort = jnp.stack([w0, w1], axis=1)
    par = jnp.where(valid == 1, (tile - t_off[tcls]) % 2, 0)
    lo = jnp.asarray(_CLS_LO)[tcls]
    hi = jnp.asarray(_CLS_HI)[tcls]
    e0 = jnp.where(par == 0, lo, hi)
    e1 = jnp.where(par == 0, hi, lo)
    step_e = jnp.stack([e0, e1], axis=1).reshape(-1)
    e_last = step_e[2 * (total - 1) + 1]
    step_e = jnp.where(jnp.repeat(valid, 2) == 1, step_e, e_last)
    return {"src": src.reshape(n_tiles, 1, tm), "dst": dst.reshape(n_tiles, 1, tm), "w": wsort,
            "step_e": step_e, "valid": valid, "flip": par, "n_tiles": n_tiles}


def _expert_kernel(tm, se_ref, va_ref, fl_ref, src_ref, dst_ref, wt_ref, h_hbm, w1_ref, w3_ref, w2_ref,
                   o_hbm, xg_ref, xb_ref, acc_ref, gsem, ssem):
    del se_ref
    i = pl.program_id(0)
    k = pl.program_id(1)

    def row_in(r):
        return pltpu.make_async_copy(h_hbm.at[pl.ds(src_ref[0, r], 1), :], xg_ref.at[pl.ds(r, 1), :], gsem)

    def row_out(r):
        return pltpu.make_async_copy(acc_ref.at[pl.ds(r, 1), :], o_hbm.at[pl.ds(dst_ref[0, r], 1), :], ssem)

    @pl.when(va_ref[i] == 1)
    def _():
        @pl.when(k == 0)
        def _():
            def start(r, c):
                row_in(r).start()
                return c
            lax.fori_loop(0, tm, start, 0, unroll=DMA_UNROLL)

            def wait(r, c):
                row_in(r).wait()
                return c
            lax.fori_loop(0, tm, wait, 0, unroll=DMA_UNROLL)
            xb_ref[...] = xg_ref[...].astype(BF16)

        x = xb_ref[...]
        h1 = jnp.dot(x, w1_ref[...], preferred_element_type=F32)
        h3 = jnp.dot(x, w3_ref[...], preferred_element_type=F32)
        a = ((h1 * jax.nn.sigmoid(h1)) * h3).astype(BF16)
        y = jnp.dot(a, w2_ref[...], preferred_element_type=F32)
        wsel = (k + fl_ref[i]) % 2
        wt = jnp.where(wsel == 0, wt_ref[:, 0:1], wt_ref[:, 1:2])

        @pl.when(k == 0)
        def _():
            acc_ref[...] = wt * y

        @pl.when(k == 1)
        def _():
            acc_ref[...] = acc_ref[...] + wt * y

            full = dst_ref[0, tm - 1] >= 0

            @pl.when(full)
            def _():
                def start(r, c):
                    row_out(r).start()
                    return c
                lax.fori_loop(0, tm, start, 0, unroll=DMA_UNROLL)

                def wait(r, c):
                    row_out(r).wait()
                    return c
                lax.fori_loop(0, tm, wait, 0, unroll=DMA_UNROLL)

            @pl.when(jnp.logical_not(full))
            def _():
                def start(r, c):
                    @pl.when(dst_ref[0, r] >= 0)
                    def _():
                        row_out(r).start()
                    return c
                lax.fori_loop(0, tm, start, 0, unroll=DMA_UNROLL)

                def wait(r, c):
                    @pl.when(dst_ref[0, r] >= 0)
                    def _():
                        row_out(r).wait()
                    return c
                lax.fori_loop(0, tm, wait, 0, unroll=DMA_UNROLL)


def _expert_call(h2, plan, w1_bf, w3_bf, w2_bf, tm):
    rows, d = h2.shape
    de = w1_bf.shape[2]
    n_tiles = plan["n_tiles"]
    grid_spec = pltpu.PrefetchScalarGridSpec(
        num_scalar_prefetch=3,
        grid=(n_tiles, 2),
        in_specs=[pl.BlockSpec((None, 1, tm), lambda i, k, se, va, fl: (i, 0, 0), memory_space=pltpu.SMEM),
                  pl.BlockSpec((None, 1, tm), lambda i, k, se, va, fl: (i, 0, 0), memory_space=pltpu.SMEM),
                  pl.BlockSpec((tm, 2), lambda i, k, se, va, fl: (i, 0)),
                  pl.BlockSpec(memory_space=pl.ANY),
                  pl.BlockSpec((None, d, de), lambda i, k, se, va, fl: (se[2 * i + k], 0, 0)),
                  pl.BlockSpec((None, d, de), lambda i, k, se, va, fl: (se[2 * i + k], 0, 0)),
                  pl.BlockSpec((None, de, d), lambda i, k, se, va, fl: (se[2 * i + k], 0, 0))],
        out_specs=pl.BlockSpec(memory_space=pl.ANY),
        scratch_shapes=[pltpu.VMEM((tm, d), F32), pltpu.VMEM((tm, d), BF16), pltpu.VMEM((tm, d), F32),
                        pltpu.SemaphoreType.DMA(()), pltpu.SemaphoreType.DMA(())])
    return pl.pallas_call(
        functools.partial(_expert_kernel, tm),
        grid_spec=grid_spec,
        out_shape=_sds((rows, d), F32),
        compiler_params=pltpu.CompilerParams(dimension_semantics=("arbitrary", "arbitrary"),
                                             vmem_limit_bytes=V7X_VMEM_LIMIT, disable_bounds_checks=True),
        name="moe_experts",
    )(plan["step_e"], plan["valid"], plan["flip"], plan["src"], plan["dst"], plan["w"], h2,
      w1_bf, w3_bf, w2_bf)


def _final_kernel(x_ref, moe_ref, g2_ref, g_ref, o_ref):
    x = x_ref[...] + g2_ref[...] * moe_ref[...]
    ms = jnp.mean(x * x, axis=-1, keepdims=True)
    o_ref[...] = x * lax.rsqrt(ms + EPS) * g_ref[...]


def _final_call(x, moe, mod3, final_g, rows, seq_len, n_batch):
    d = x.shape[1]
    tm = _tile(math.gcd(seq_len, rows), 512)
    return pl.pallas_call(
        _final_kernel,
        grid=(rows // tm,),
        in_specs=[pl.BlockSpec((tm, d), lambda i: (i, 0)),
                  pl.BlockSpec((tm, d), lambda i: (i, 0)),
                  pl.BlockSpec((None, 1, d), lambda i: (jnp.minimum((i * tm) // seq_len, n_batch), 0, 5)),
                  pl.BlockSpec((1, d), lambda i: (0, 0))],
        out_specs=pl.BlockSpec((tm, d), lambda i: (i, 0)),
        out_shape=_sds((rows, d), F32),
        compiler_params=_params(("parallel",)),
        name="final_norm",
    )(x, moe, mod3, final_g.reshape(1, d))


def kernel(x, c, ctx, c_ctx, ada_w, ada_b, norm1_g, norm2_g, w_in, conv_a_w, conv_a_b, lru_wa, lru_ba, lru_wx, lru_bx, lru_lam, sgu_norm_g, sgu_ws, sgu_bs, hy_conv_w, hy_conv_b, hy_w1, hy_b1, hy_w2, hy_b2, hy_w3, hy_freq, hy_skip, w_br, w_out, exp_w1, exp_w3, exp_w2, router_w, router_b, final_g):
    n_batch, n_lat, d = x.shape
    n_ctx = ctx.shape[1]
    depth = ada_w.shape[0]
    w = d // 2
    t_lat = n_batch * n_lat
    t_ctx = n_batch * n_ctx
    t_all = t_lat + t_ctx
    heads, bw = lru_wa.shape[2], lru_wa.shape[3]
    groups = sgu_ws.shape[1]
    emb = hy_w1.shape[1]
    assert n_batch < MOD_ROWS and router_w.shape[1] == N_GROUPS * EXP_PER_GROUP
    assert n_lat % SGU_CHUNK == 0 and n_ctx % SGU_CHUNK == 0

    xs = jnp.concatenate([x.reshape(t_lat, d), ctx.reshape(t_ctx, d)], axis=0)
    c_all = jnp.zeros((MOD_ROWS, d), F32).at[:n_batch].set(c).at[n_batch].set(c_ctx)
    mod = _mod_all(c_all, ada_w, ada_b)

    consts_lat = _dft_consts(n_lat, w, emb)
    consts_ctx = _dft_consts(n_ctx, w, emb)
    rw_t = router_w.T
    tm_e = _tile(math.gcd(t_lat, t_ctx), MOE_TILE)

    moe = None
    mod3_prev = None
    for l in range(depth):
        last = l == depth - 1
        rows = t_lat if last else t_all
        mod3 = mod[l].reshape(MOD_ROWS, 1, 6 * d)
        z = _win_call(xs, moe, mod3, mod3_prev, norm1_g[l], w_in[l].astype(BF16), t_all, n_lat, n_batch)

        lw = {"conv_w": conv_a_w[l], "conv_b": conv_a_b[l].reshape(1, w),
              "wg": jnp.concatenate([lru_wa[l], lru_wx[l]], axis=-1).astype(BF16),
              "bg": jnp.concatenate([lru_ba[l].reshape(2, heads, 1, bw), lru_bx[l].reshape(2, heads, 1, bw)], axis=-1),
              "lam": lru_lam[l].reshape(2, 1, w)}
        hlru = _lru_call(z, lw, n_lat, n_ctx, n_batch)

        bias_full = jnp.repeat(sgu_bs[l].T, w // groups, axis=1)
        yb = _sgu_call(z, sgu_norm_g[l], sgu_ws[l].astype(BF16), bias_full, rows)

        hw = {"conv_w": hy_conv_w[l], "conv_b": hy_conv_b[l],
              "w1": jnp.pad(hy_w1[l], ((0, 128 - emb), (0, 0))), "b1": hy_b1[l][None, :],
              "w2": hy_w2[l], "b2": hy_b2[l][None, :], "w3": hy_w3[l], "freq": hy_freq[l][None, :],
              "skip3": hy_skip[l].reshape(hy_skip.shape[1], 1, w)}
        yc_lat = _hyena_call(z, consts_lat, hw, 0, n_lat, n_batch)
        yc_ctx = yc_lat if last else _hyena_call(z, consts_ctx, hw, t_lat, n_ctx, n_batch)

        xs = _merge_call(hlru, z, yb, yc_lat, yc_ctx, xs, moe, mod3, mod3_prev, w_br[l].astype(BF16),
                         w_out[l].astype(BF16), rows, n_lat, n_batch)

        h2, cls, wts = _router_call(xs, mod3, norm2_g[l], rw_t, router_b, rows, n_lat, n_batch)
        plan = _moe_plan(cls[0], wts, tm_e)
        moe = _expert_call(h2, plan, exp_w1[l].astype(BF16), exp_w3[l].astype(BF16),
                           exp_w2[l].astype(BF16), tm_e)
        mod3_prev = mod3

    out = _final_call(xs, moe, mod3_prev, final_g, t_lat, n_lat, n_batch)
    return out.reshape(n_batch, n_lat, d)
```

```python
import functools
import math

import jax
import jax.numpy as jnp
import numpy as np
from jax import lax
from jax.experimental import pallas as pl
from jax.experimental.pallas import tpu as pltpu

F32 = jnp.float32
BF16 = jnp.bfloat16
I32 = jnp.int32

EPS = 1e-6
LRU_C = 8.0
SGU_CHUNK = 128
HY_TARGET = 1e-2
HY_FAST_PCT = 0.3
HY_SLOW_PCT = 1.5
HY_MAX_DECAY = math.log(HY_TARGET) / HY_FAST_PCT
HY_MIN_DECAY = math.log(HY_TARGET) / HY_SLOW_PCT
N_GROUPS = 4
EXP_PER_GROUP = 4
N_PAIRS = 6
N_CLASSES = N_GROUPS * N_PAIRS
MOD_ROWS = 16
V7X_VMEM_LIMIT = 56 * 1024 * 1024
PROLOGUE_ROWS = 128
HALO = 16
MOE_TILE = 512
DMA_UNROLL = 8


def _params(sem, vmem=V7X_VMEM_LIMIT):
    return pltpu.CompilerParams(dimension_semantics=sem, vmem_limit_bytes=vmem)


def _sds(shape, dtype):
    return jax.ShapeDtypeStruct(shape, dtype)


def _tile(n, pref):
    t = min(pref, n)
    while n % t:
        t //= 2
    return t


def _modulate(x, g, shift, scale):
    ms = jnp.mean(x * x, axis=-1, keepdims=True)
    y = x * lax.rsqrt(ms + EPS) * g
    return y * (1.0 + scale) + shift


def _mod_kernel(c_ref, w_ref, b_ref, o_ref):
    c = c_ref[...]
    s = (c * jax.nn.sigmoid(c)).astype(BF16)
    o_ref[...] = jnp.dot(s, w_ref[...].astype(BF16), preferred_element_type=F32) + b_ref[...]


def _mod_all(c_all, ada_w, ada_b):
    depth, d, n = ada_w.shape
    tn = _tile(n, 1024)
    return pl.pallas_call(
        _mod_kernel,
        grid=(depth, n // tn),
        in_specs=[pl.BlockSpec((MOD_ROWS, d), lambda l, j: (0, 0)),
                  pl.BlockSpec((None, d, tn), lambda l, j: (l, 0, j)),
                  pl.BlockSpec((None, 1, tn), lambda l, j: (l, 0, j))],
        out_specs=pl.BlockSpec((None, MOD_ROWS, tn), lambda l, j: (l, 0, j)),
        out_shape=_sds((depth, MOD_ROWS, n), F32),
        compiler_params=_params(("parallel", "parallel")),
        name="adaln_mod",
    )(c_all, ada_w, ada_b.reshape(depth, 1, n))


def _win_kernel(fuse_moe, *refs):
    if fuse_moe:
        x_ref, moe_ref, g2_ref, sh_ref, sc_ref, g_ref, w_ref, o_ref, h_ref = refs
    else:
        x_ref, sh_ref, sc_ref, g_ref, w_ref, o_ref, h_ref = refs

    @pl.when(pl.program_id(1) == 0)
    def _():
        def chunk(c, carry):
            rs = pl.ds(pl.multiple_of(c * PROLOGUE_ROWS, PROLOGUE_ROWS), PROLOGUE_ROWS)
            x = x_ref[rs, :]
            if fuse_moe:
                x = x + g2_ref[...] * moe_ref[rs, :]
            h_ref[rs, :] = _modulate(x, g_ref[...], sh_ref[...], sc_ref[...]).astype(BF16)
            return carry
        lax.fori_loop(0, x_ref.shape[0] // PROLOGUE_ROWS, chunk, 0)

    o_ref[...] = jnp.dot(h_ref[...], w_ref[...], preferred_element_type=F32).astype(o_ref.dtype)


def _win_call(x, moe, mod3, mod3_prev, norm_g, w_bf, rows, seq_len, n_batch):
    d = x.shape[1]
    n = w_bf.shape[1]
    tm = _tile(math.gcd(seq_len, rows), 1024)
    tn = d // 2
    fuse = moe is not None

    def mrow(i):
        return jnp.minimum((i * tm) // seq_len, n_batch)

    x_spec = pl.BlockSpec((tm, d), lambda i, j: (i, 0))

    def mod_spec(col):
        return pl.BlockSpec((None, 1, d), lambda i, j: (mrow(i), 0, col))

    in_specs = [x_spec]
    args = [x]
    if fuse:
        in_specs += [pl.BlockSpec((tm, d), lambda i, j: (i, 0), pipeline_mode=pl.Buffered(1)), mod_spec(5)]
        args += [moe, mod3_prev]
    in_specs += [mod_spec(0), mod_spec(1), pl.BlockSpec((1, d), lambda i, j: (0, 0)),
                 pl.BlockSpec((d, tn), lambda i, j: (0, j))]
    args += [mod3, mod3, norm_g.reshape(1, d), w_bf]
    return pl.pallas_call(
        functools.partial(_win_kernel, fuse),
        grid=(rows // tm, n // tn),
        in_specs=in_specs,
        out_specs=pl.BlockSpec((tm, tn), lambda i, j: (i, j)),
        out_shape=_sds((x.shape[0], n), BF16),
        scratch_shapes=[pltpu.VMEM((tm, d), BF16)],
        compiler_params=_params(("parallel", "arbitrary")),
        name="in_proj",
    )(*args)


def _lru_tile(d, j, nt_c, nt_l):
    is_ctx = j < nt_c
    nt = jnp.where(is_ctx, nt_c, nt_l)
    jl = jnp.where(is_ctx, j, j - nt_c)
    return is_ctx, jl + d * (nt - 1 - 2 * jl), nt


def _lru_kernel(tl, nt_c, nt_l, heads, xm_ref, xp_ref, xn_ref, cw_ref, cb_ref, wg_ref, bg_ref, lam_ref,
                o_ref, pad_ref, a_ref, b_ref, hs_ref, hc_ref):
    d = pl.program_id(0)
    j = pl.program_id(2)
    _, jt, nt = _lru_tile(d, j, nt_c, nt_l)
    w = a_ref.shape[1]
    bw = w // heads

    pad_ref[8:8 + tl, :] = xm_ref[...].astype(F32)
    prev = xp_ref[...].astype(F32)[HALO - 8:HALO]
    nxt = xn_ref[...].astype(F32)[0:8]
    pad_ref[0:8, :] = jnp.where(jt > 0, prev, 0.0)
    pad_ref[8 + tl:16 + tl, :] = jnp.where(jt < nt - 1, nxt, 0.0)
    cw = cw_ref[...]
    u = (cw[0:1] * pad_ref[7:7 + tl, :] + cw[1:2] * pad_ref[8:8 + tl, :]
         + cw[2:3] * pad_ref[9:9 + tl, :] + cw[3:4] * pad_ref[10:10 + tl, :]) + cb_ref[...]
    ub = u.astype(BF16)
    lam = lam_ref[...]
    sp = jnp.maximum(-lam, 0.0) + jnp.log1p(jnp.exp(-jnp.abs(lam)))
    for hd in range(heads):
        cs = slice(hd * bw, (hd + 1) * bw)
        g = jnp.dot(ub[:, cs], wg_ref[hd], preferred_element_type=F32) + bg_ref[hd]
        r = jax.nn.sigmoid(g[:, :bw])
        i = jax.nn.sigmoid(g[:, bw:])
        log_a = (-LRU_C) * r * sp[:, cs]
        a = jnp.exp(log_a)
        a_ref[:, cs] = a
        b_ref[:, cs] = jnp.sqrt(1.0 - a * a) * (i * u[:, cs])

    @pl.when(j == 0)
    def _():
        hc_ref[...] = jnp.zeros_like(hc_ref)

    def step(s, h):
        t = s + d * (tl - 1 - 2 * s)
        h = a_ref[pl.ds(t, 1), :] * h + b_ref[pl.ds(t, 1), :]
        hs_ref[pl.ds(t, 1), :] = h
        return h

    h = lax.fori_loop(0, tl, step, hc_ref[...], unroll=8)
    hc_ref[...] = h
    o_ref[...] = hs_ref[...].astype(o_ref.dtype)


def _lru_call(z, lw, n_lat, n_ctx, n_batch):
    w = lw["conv_w"].shape[1]
    heads = lw["wg"].shape[1]
    tl = _tile(math.gcd(n_lat, n_ctx), 512)
    nt_c, nt_l = n_ctx // tl, n_lat // tl
    ctx_blk0 = n_batch * nt_l
    per = tl // HALO
    last_h = n_batch * (n_lat + n_ctx) // HALO - 1

    def blk(d, b, j):
        is_ctx, jt, _ = _lru_tile(d, j, nt_c, nt_l)
        return jnp.where(is_ctx, ctx_blk0 + b * nt_c + jt, b * nt_l + jt)

    in_specs = [pl.BlockSpec((tl, w), lambda d, b, j: (blk(d, b, j), 0)),
                pl.BlockSpec((HALO, w), lambda d, b, j: (jnp.maximum(blk(d, b, j) * per - 1, 0), 0)),
                pl.BlockSpec((HALO, w), lambda d, b, j: (jnp.minimum((blk(d, b, j) + 1) * per, last_h), 0)),
                pl.BlockSpec((4, w), lambda d, b, j: (0, 0)),
                pl.BlockSpec((1, w), lambda d, b, j: (0, 0)),
                pl.BlockSpec((None, heads, w // heads, 2 * (w // heads)), lambda d, b, j: (d, 0, 0, 0)),
                pl.BlockSpec((None, heads, 1, 2 * (w // heads)), lambda d, b, j: (d, 0, 0, 0)),
                pl.BlockSpec((None, 1, w), lambda d, b, j: (d, 0, 0))]
    return pl.pallas_call(
        functools.partial(_lru_kernel, tl, nt_c, nt_l, heads),
        grid=(2, n_batch, nt_c + nt_l),
        in_specs=in_specs,
        out_specs=pl.BlockSpec((None, tl, w), lambda d, b, j: (d, blk(d, b, j), 0)),
        out_shape=_sds((2, n_batch * (n_lat + n_ctx), w), BF16),
        scratch_shapes=[pltpu.VMEM((tl + 16, w), F32), pltpu.VMEM((tl, w), F32),
                        pltpu.VMEM((tl, w), F32), pltpu.VMEM((tl, w), F32), pltpu.VMEM((1, w), F32)],
        compiler_params=_params(("arbitrary", "arbitrary", "arbitrary")),
        name="rglru_scan",
    )(z, z, z, lw["conv_w"], lw["conv_b"], lw["wg"], lw["bg"], lw["lam"])


def _sgu_kernel(groups, u_ref, v_ref, g_ref, ws_ref, bias_ref, o_ref):
    v = v_ref[...].astype(F32)
    mu = jnp.mean(v, axis=-1, keepdims=True)
    vc = v - mu
    var = jnp.mean(vc * vc, axis=-1, keepdims=True)
    vn = (vc * lax.rsqrt(var + EPS) * g_ref[...]).astype(BF16)
    ts, w = v.shape
    gw = w // groups
    for ck in range(ts // SGU_CHUNK):
        rs = slice(ck * SGU_CHUNK, (ck + 1) * SGU_CHUNK)
        for g in range(groups):
            cs = slice(g * gw, (g + 1) * gw)
            mixed = jnp.dot(ws_ref[g], vn[rs, cs], preferred_element_type=F32) + bias_ref[:, cs]
            o_ref[rs, cs] = (u_ref[rs, cs].astype(F32) * mixed).astype(o_ref.dtype)


def _sgu_call(z, norm_g, ws_bf, bias_full, rows):
    groups = ws_bf.shape[0]
    w = norm_g.shape[0]
    ts = _tile(rows, 512)
    return pl.pallas_call(
        functools.partial(_sgu_kernel, groups),
        grid=(rows // ts,),
        in_specs=[pl.BlockSpec((ts, w), lambda i: (i, 2)),
                  pl.BlockSpec((ts, w), lambda i: (i, 3)),
                  pl.BlockSpec((1, w), lambda i: (0, 0)),
                  pl.BlockSpec((groups, SGU_CHUNK, SGU_CHUNK), lambda i: (0, 0, 0)),
                  pl.BlockSpec((SGU_CHUNK, w), lambda i: (0, 0))],
        out_specs=pl.BlockSpec((ts, w), lambda i: (i, 0)),
        out_shape=_sds((rows, w), BF16),
        compiler_params=_params(("parallel",)),
        name="sgu",
    )(z, z, norm_g.reshape(1, w), ws_bf, bias_full)


def _hconv_kernel(tl, nt, xm_ref, xp_ref, xn_ref, cw_ref, cb_ref, o_ref, pad_ref):
    j = pl.program_id(2)
    pad_ref[8:8 + tl, :] = xm_ref[...].astype(F32)
    prev = xp_ref[...].astype(F32)[HALO - 8:HALO]
    nxt = xn_ref[...].astype(F32)[0:8]
    pad_ref[0:8, :] = jnp.where(j > 0, prev, 0.0)
    pad_ref[8 + tl:16 + tl, :] = jnp.where(j < nt - 1, nxt, 0.0)
    cw = cw_ref[...]
    u = (cw[0:1] * pad_ref[7:7 + tl, :] + cw[1:2] * pad_ref[8:8 + tl, :]
         + cw[2:3] * pad_ref[9:9 + tl, :]) + cb_ref[...]
    o_ref[...] = u.astype(o_ref.dtype)


def _hconv_call(z, conv_w, conv_b, row0, seq_len, n_batch):
    w = conv_w.shape[1] // 3
    tl = _tile(seq_len, 512)
    nt = seq_len // tl
    blk0 = row0 // tl
    hb0 = row0 // HALO
    per = tl // HALO
    last_h = (row0 + n_batch * seq_len) // HALO - 1
    return pl.pallas_call(
        functools.partial(_hconv_kernel, tl, nt),
        grid=(3, n_batch, nt),
        in_specs=[pl.BlockSpec((tl, w), lambda s, b, j: (blk0 + b * nt + j, 4 + s)),
                  pl.BlockSpec((HALO, w), lambda s, b, j: (jnp.maximum(hb0 + (b * nt + j) * per - 1, 0), 4 + s)),
                  pl.BlockSpec((HALO, w), lambda s, b, j: (jnp.minimum(hb0 + (b * nt + j + 1) * per, last_h), 4 + s)),
                  pl.BlockSpec((3, w), lambda s, b, j: (0, s)),
                  pl.BlockSpec((1, w), lambda s, b, j: (0, s))],
        out_specs=pl.BlockSpec((None, tl, w), lambda s, b, j: (s, b * nt + j, 0)),
        out_shape=_sds((3, n_batch * seq_len, w), BF16),
        scratch_shapes=[pltpu.VMEM((tl + 16, w), F32)],
        compiler_params=_params(("parallel", "parallel", "arbitrary")),
        name="hyena_conv",
    )(z, z, z, conv_w, conv_b.reshape(1, 3 * w))


def _hfilt_kernel(feat_ref, t_ref, dl_ref, w1_ref, b1_ref, w2_ref, b2_ref, w3_ref, fr_ref, o_ref):
    hi = lax.Precision.HIGHEST
    fr = fr_ref[...]
    h = jnp.sin(fr * (jnp.dot(feat_ref[...], w1_ref[...], precision=hi, preferred_element_type=F32) + b1_ref[...]))
    h = jnp.sin(fr * (jnp.dot(h, w2_ref[...], precision=hi, preferred_element_type=F32) + b2_ref[...]))
    h = jnp.dot(h, w3_ref[...], precision=hi, preferred_element_type=F32)
    w = h.shape[1] // 2
    h = h * jnp.exp(-t_ref[...] * dl_ref[...])
    fwd = h[:, :w]
    row = lax.broadcasted_iota(I32, (h.shape[0], 1), 0) + pl.program_id(1) * h.shape[0]
    bwd = jnp.where(row > 0, h[:, w:], 0.0)
    o_ref[:, :w] = (fwd + bwd).astype(o_ref.dtype)
    o_ref[:, w:] = (fwd - bwd).astype(o_ref.dtype)


def _hfilt_call(consts, hw):
    feat, tcol, deltas = consts["feat"], consts["t"], consts["deltas"]
    seq_len, fpad = feat.shape
    hid = hw["w2"].shape[0]
    w = deltas.shape[1]
    orders = hw["w3"].shape[1] // (2 * w)
    tr = _tile(seq_len, 512)
    return pl.pallas_call(
        _hfilt_kernel,
        grid=(orders, seq_len // tr),
        in_specs=[pl.BlockSpec((tr, fpad), lambda o, i: (i, 0)),
                  pl.BlockSpec((tr, 1), lambda o, i: (i, 0)),
                  pl.BlockSpec((1, 2 * w), lambda o, i: (0, 0)),
                  pl.BlockSpec((fpad, hid), lambda o, i: (0, 0)),
                  pl.BlockSpec((1, hid), lambda o, i: (0, 0)),
                  pl.BlockSpec((hid, hid), lambda o, i: (0, 0)),
                  pl.BlockSpec((1, hid), lambda o, i: (0, 0)),
                  pl.BlockSpec((hid, 2 * w), lambda o, i: (0, o)),
                  pl.BlockSpec((1, hid), lambda o, i: (0, 0))],
        out_specs=pl.BlockSpec((None, tr, 2 * w), lambda o, i: (o, i, 0)),
        out_shape=_sds((orders, seq_len, 2 * w), BF16),
        compiler_params=_params(("parallel", "parallel")),
        name="hyena_filter",
    )(feat, tcol, jnp.concatenate([deltas, deltas], axis=1), hw["w1"], hw["b1"], hw["w2"], hw["b2"],
      hw["w3"], hw["freq"])


_R8 = math.sqrt(0.5)
_FWD_COEF = np.array([[1, 1, 1, 1, 1, -1, 1, -1],
                      [1, _R8, 0, -_R8, 0, -_R8, -1, -_R8],
                      [1, 0, -1, 0, 0, -1, 0, 1],
                      [1, -_R8, 0, _R8, 0, -_R8, 1, -_R8]],
                     np.float32)
N_FAM = 4


def _radix8_inputs(coef_ref, g, x_ref, a_ref, n2, rows):
    def chunk(c, carry):
        r0 = pl.multiple_of(c * rows, rows)
        xs = [x_ref[pl.ds(t1 * n2 + r0, rows), :].astype(F32) for t1 in range(4)]
        for half in range(2):
            acc = coef_ref[g, 4 * half] * xs[0]
            for t1 in range(1, 4):
                acc = acc + coef_ref[g, 4 * half + t1] * xs[t1]
            a_ref[pl.ds(half * n2 + r0, rows), :] = acc.astype(a_ref.dtype)
        return carry
    lax.fori_loop(0, n2 // rows, chunk, 0)


def _hspec_kernel(tk, n2, rows, coef_ref, d_ref, p_ref, q_ref, ma_ref, mb_ref, ap_ref, aq_ref):
    g = pl.program_id(0)
    m = pl.program_id(3)

    @pl.when(m == 0)
    def _():
        _radix8_inputs(coef_ref, g, p_ref, ap_ref, n2, rows)
        _radix8_inputs(coef_ref, g, q_ref, aq_ref, n2, rows)

    re = jnp.dot(d_ref[:tk, :], ap_ref[...], preferred_element_type=F32)
    kfi = jnp.dot(d_ref[tk:, :], aq_ref[...], preferred_element_type=F32)
    first = (g == 0) & (m == 0)
    kfi = jnp.where((lax.broadcasted_iota(I32, (tk, 1), 0) == 0) & first, 0.0, kfi)
    ma_ref[:tk, :] = re
    ma_ref[tk:, :] = re
    mb_ref[:tk, :] = -kfi
    mb_ref[tk:, :] = kfi

    @pl.when(first)
    def _():
        nyq = jnp.dot(d_ref[tk:tk + HALO, :], ap_ref[...], preferred_element_type=F32)
        row0 = lax.broadcasted_iota(I32, (HALO, 1), 0) == 0
        ma_ref[tk:tk + HALO, :] = jnp.where(row0, nyq, re[:HALO])


def _hspec_call(consts, pq):
    orders, seq_len, w2 = pq.shape
    w = w2 // 2
    n2, tk = consts["n2"], consts["tk"]
    tn = _tile(w, 512)
    nn = w // tn
    rows = _tile(n2, 256)
    grid_spec = pltpu.PrefetchScalarGridSpec(
        num_scalar_prefetch=0,
        grid=(N_FAM, orders, nn, 2),
        in_specs=[pl.BlockSpec(memory_space=pltpu.SMEM),
                  pl.BlockSpec((None, 2 * tk, 2 * n2), lambda g, o, n, m: (g, m, 0)),
                  pl.BlockSpec((None, seq_len, tn), lambda g, o, n, m: (o, 0, n)),
                  pl.BlockSpec((None, seq_len, tn), lambda g, o, n, m: (o, 0, nn + n))],
        out_specs=[pl.BlockSpec((None, 2 * tk, tn), lambda g, o, n, m: (o, 2 * g + m, n)),
                   pl.BlockSpec((None, 2 * tk, tn), lambda g, o, n, m: (o, 2 * g + m, n))],
        scratch_shapes=[pltpu.VMEM((2 * n2, tn), BF16), pltpu.VMEM((2 * n2, tn), BF16)])
    return pl.pallas_call(
        functools.partial(_hspec_kernel, tk, n2, rows),
        grid_spec=grid_spec,
        out_shape=[_sds((orders, 2 * seq_len, w), F32), _sds((orders, 2 * seq_len, w), F32)],
        compiler_params=_params(("parallel", "parallel", "parallel", "arbitrary")),
        name="hyena_filter_spectrum",
    )(consts["coef"], consts["dfwd"], pq, pq)


def _hfwd_kernel(tk, n2, rows, coef_ref, d_ref, x_ref, ma_ref, mb_ref, o_ref, a_ref):
    g = pl.program_id(0)
    b = pl.program_id(3)

    @pl.when(pl.program_id(2) == 0)
    def _():
        _radix8_inputs(coef_ref, g, x_ref, a_ref.at[b], n2, rows)

    s = jnp.dot(d_ref[...], a_ref[b], preferred_element_type=F32)
    sw = jnp.concatenate([s[tk:], s[:tk]], axis=0)
    o_ref[...] = (s * ma_ref[...] + sw * mb_ref[...]).astype(o_ref.dtype)


def _hfwd_call(consts, x2d, xblk0, ma, mb, order, n_batch):
    n2, tk = consts["n2"], consts["tk"]
    seq_len = 4 * n2
    w = x2d.shape[1]
    tn = _tile(w, 512)
    rows = _tile(n2, 256)
    grid_spec = pltpu.PrefetchScalarGridSpec(
        num_scalar_prefetch=0,
        grid=(N_FAM, w // tn, 2, n_batch),
        in_specs=[pl.BlockSpec(memory_space=pltpu.SMEM),
                  pl.BlockSpec((None, 2 * tk, 2 * n2), lambda g, n, m, b: (g, m, 0)),
                  pl.BlockSpec((seq_len, tn), lambda g, n, m, b: (xblk0 + jnp.where(m == 0, b, n_batch - 1), n)),
                  pl.BlockSpec((None, 2 * tk, tn), lambda g, n, m, b: (order, 2 * g + m, n)),
                  pl.BlockSpec((None, 2 * tk, tn), lambda g, n, m, b: (order, 2 * g + m, n))],
        out_specs=pl.BlockSpec((None, 2 * tk, tn), lambda g, n, m, b: (b, 2 * g + m, n)),
        scratch_shapes=[pltpu.VMEM((n_batch, 2 * n2, tn), BF16)])
    return pl.pallas_call(
        functools.partial(_hfwd_kernel, tk, n2, rows),
        grid_spec=grid_spec,
        out_shape=_sds((n_batch, 2 * seq_len, w), BF16),
        compiler_params=_params(("parallel", "arbitrary", "arbitrary", "arbitrary")),
        name="hyena_dft",
    )(consts["coef"], consts["dfwd"], x2d, ma, mb)


def _hinv_kernel(n2, tt, e_ref, p_ref, v_ref, x_ref, sk_ref, o_ref):
    top, bot = [], []
    for g in range(N_FAM):
        r = jnp.dot(e_ref[g], p_ref[g * 2 * n2:(g + 1) * 2 * n2, :], preferred_element_type=F32)
        top.append(r[:tt])
        bot.append(r[tt:])
    ys = (top[0] + bot[0] + top[1] + top[2] + top[3],
          top[0] - bot[0] + _R8 * (top[1] - bot[1]) - bot[2] - _R8 * (top[3] + bot[3]),
          top[0] + bot[0] - bot[1] - top[2] + bot[3],
          top[0] - bot[0] - _R8 * (top[1] + bot[1]) + bot[2] + _R8 * (top[3] - bot[3]))
    for t1 in range(4):
        v = v_ref[t1].astype(F32)
        o_ref[t1] = (x_ref[t1].astype(F32) * (ys[t1] + v * sk_ref[...])).astype(o_ref.dtype)


def _hinv_call(consts, p, v4, vblk0, x4, xblk0, skip3, order, n_batch):
    n2, tt = consts["n2"], consts["tt"]
    w = v4.shape[3]
    tn = _tile(w, 512)
    blk = lambda off: pl.BlockSpec((None, 4, tt, tn), lambda j, b, n: (off + b, 0, j, n))
    return pl.pallas_call(
        functools.partial(_hinv_kernel, n2, tt),
        grid=(n2 // tt, n_batch, w // tn),
        in_specs=[pl.BlockSpec((N_FAM, 2 * tt, 2 * n2), lambda j, b, n: (0, j, 0)),
                  pl.BlockSpec((None, 8 * n2, tn), lambda j, b, n: (b, 0, n)),
                  blk(vblk0), blk(xblk0),
                  pl.BlockSpec((None, 1, tn), lambda j, b, n: (order, 0, n))],
        out_specs=blk(0),
        out_shape=_sds((n_batch, 4, n2, w), BF16),
        compiler_params=_params(("parallel", "arbitrary", "arbitrary")),
        name="hyena_idft",
    )(consts["einv"], p, v4, x4, skip3)


def _dft_consts(seq_len, w, emb):
    f32 = F32
    n2 = seq_len // 4
    tk = n2 // 2
    tt = _tile(n2, 256)
    nfull = 2 * seq_len
    t2 = jnp.arange(n2, dtype=I32)

    def cs_table(rows):
        ang = ((rows[:, None] * t2[None, :]) % nfull).astype(f32) * (math.pi / seq_len)
        return jnp.cos(ang), jnp.sin(ang)

    cb, sb = cs_table(8 * t2)
    co, so = cs_table(jnp.array([0, 4, 1, 2, 3], I32))
    c = co[:, None, :] * cb[None, :, :] - so[:, None, :] * sb[None, :, :]
    s = so[:, None, :] * cb[None, :, :] + co[:, None, :] * sb[None, :, :]
    nyq = jnp.where(t2 % 2 == 0, 1.0, -1.0).astype(f32)
    zero = jnp.zeros((tk, n2), f32)

    def tiled_rows(re, im):
        nb = re.shape[0] // tk
        return jnp.stack([re.reshape(nb, tk, -1), im.reshape(nb, tk, -1)], axis=1).reshape(2 * re.shape[0], -1)

    c0, s0, c4, s4 = c[0, :tk], s[0, :tk], c[1, :tk], s[1, :tk]
    dfwd = [jnp.concatenate([jnp.concatenate([c0, zero], axis=1),
                             jnp.concatenate([(-s0).at[0].set(nyq), zero], axis=1),
                             jnp.concatenate([zero, c4], axis=1),
                             jnp.concatenate([zero, -s4], axis=1)], axis=0)]
    for g in range(1, N_FAM):
        cg, sg = c[g + 1], s[g + 1]
        dfwd.append(tiled_rows(jnp.concatenate([cg, sg], axis=1), jnp.concatenate([-sg, cg], axis=1)))
    dfwd = jnp.stack(dfwd).astype(BF16)

    def tiled_cols(re, im):
        nb = re.shape[1] // tk
        return jnp.stack([re.reshape(n2, nb, tk), im.reshape(n2, nb, tk)], axis=2).reshape(n2, 2 * re.shape[1])

    sc = 2.0 / nfull
    zcol = jnp.zeros((n2, 2 * tk), f32)
    top0 = jnp.concatenate([(c0.T * sc).at[:, 0].multiply(0.5), (-s0.T * sc).at[:, 0].set(nyq / nfull), zcol], axis=1)
    bot0 = jnp.concatenate([zcol, c4.T * sc, -s4.T * sc], axis=1)
    tops, bots = [top0], [bot0]
    for g in range(1, N_FAM):
        cg, sg = c[g + 1].T * sc, s[g + 1].T * sc
        tops.append(tiled_cols(cg, -sg))
        bots.append(tiled_cols(sg, cg))
    einv = jnp.stack([jnp.stack([tp.reshape(n2 // tt, tt, 2 * n2), bt.reshape(n2 // tt, tt, 2 * n2)], axis=1)
                      .reshape(2 * n2, 2 * n2) for tp, bt in zip(tops, bots)]).astype(BF16)
    bands = (emb - 1) // 2
    t = jnp.linspace(0.0, 1.0, seq_len, dtype=f32)[:, None]
    wv = 2.0 * math.pi * jnp.arange(seq_len, dtype=f32)[:, None] / seq_len
    fb = jnp.linspace(1e-4, bands - 1, bands, dtype=f32)[None, :]
    feat = jnp.concatenate([t, jnp.cos(fb * wv), -jnp.sin(fb * wv)], axis=-1)
    fpad = 128
    feat = jnp.pad(feat, ((0, 0), (0, fpad - emb)))
    deltas = jnp.abs(jnp.linspace(HY_MIN_DECAY, HY_MAX_DECAY, w, dtype=f32))[None, :]
    return {"dfwd": dfwd, "einv": einv, "coef": jnp.asarray(_FWD_COEF), "feat": feat, "t": t, "deltas": deltas,
            "n2": n2, "tk": tk, "tt": tt}


def _hyena_call(z, consts, hw, row0, seq_len, n_batch):
    n2 = consts["n2"]
    w = hw["skip3"].shape[2]
    hc = _hconv_call(z, hw["conv_w"], hw["conv_b"], row0, seq_len, n_batch)
    hc2 = hc.reshape(3 * n_batch * seq_len, w)
    hc4 = hc.reshape(3 * n_batch, 4, n2, w)
    pq = _hfilt_call(consts, hw)
    ma, mb = _hspec_call(consts, pq)
    p0 = _hfwd_call(consts, hc2, 0, ma, mb, 0, n_batch)
    s1 = _hinv_call(consts, p0, hc4, 0, hc4, n_batch, hw["skip3"], 0, n_batch)
    p1 = _hfwd_call(consts, s1.reshape(n_batch * seq_len, w), 0, ma, mb, 1, n_batch)
    yc = _hinv_call(consts, p1, s1, 0, hc4, 2 * n_batch, hw["skip3"], 1, n_batch)
    return yc.reshape(n_batch * seq_len, w)


def _merge_kernel(lat_tiles, fuse_moe, hf_ref, hb_ref, ga_ref, yb_ref, ycl_ref, ycc_ref, g0a, g0b, g1a, g1b, g2a, g2b,
                  x_ref, gate_ref, wbr_ref, wout_ref, *rest):
    if fuse_moe:
        moe_ref, mg_ref, o_ref, m_ref = rest
    else:
        o_ref, m_ref = rest
    ga = ga_ref[...].astype(F32)
    ya = ((hf_ref[...].astype(F32) + hb_ref[...].astype(F32)) * jax.nn.gelu(ga)).astype(BF16)
    yc = jnp.where(pl.program_id(0) < lat_tiles, ycl_ref[...], ycc_ref[...])
    ys = (ya, yb_ref[...], yc)
    w = ya.shape[1]
    gates = ((g0a, g0b), (g1a, g1b), (g2a, g2b))
    for half in range(2):
        cs = slice(half * w, (half + 1) * w)
        acc = None
        for br in range(3):
            gt = jax.nn.sigmoid(gates[br][half][...].astype(F32))
            t = gt * jnp.dot(ys[br], wbr_ref[br, :, cs], preferred_element_type=F32)
            acc = t if acc is None else acc + t
        m_ref[:, cs] = acc.astype(BF16)
    out = jnp.dot(m_ref[...], wout_ref[...], preferred_element_type=F32)
    x = x_ref[...]
    if fuse_moe:
        x = x + mg_ref[...] * moe_ref[...]
    o_ref[...] = x + gate_ref[...] * out


def _merge_call(hlru, z, yb, yc_lat, yc_ctx, x, moe, mod3, mod3_prev, wbr_bf, wout_bf, rows, seq_len, n_batch):
    d = x.shape[1]
    fuse = moe is not None
    w = d // 2
    tm = _tile(math.gcd(seq_len, rows), 256)
    lat_tiles = yc_lat.shape[0] // tm
    ctx_last = yc_ctx.shape[0] // tm - 1

    def mrow(i):
        return jnp.minimum((i * tm) // seq_len, n_batch)

    def zcol(c):
        return pl.BlockSpec((tm, w), lambda i: (i, c))

    const = dict(pipeline_mode=pl.Buffered(1))
    in_specs = [pl.BlockSpec((None, tm, w), lambda i: (0, i, 0)),
                pl.BlockSpec((None, tm, w), lambda i: (1, i, 0)),
                zcol(1), zcol(0),
                pl.BlockSpec((tm, w), lambda i: (jnp.minimum(i, lat_tiles - 1), 0)),
                pl.BlockSpec((tm, w), lambda i: (jnp.clip(i - lat_tiles, 0, ctx_last), 0))] + [
                zcol(7 + c) for c in range(6)] + [
                pl.BlockSpec((tm, d), lambda i: (i, 0)),
                pl.BlockSpec((None, 1, d), lambda i: (mrow(i), 0, 2)),
                pl.BlockSpec((3, w, d), lambda i: (0, 0, 0), **const),
                pl.BlockSpec((d, d), lambda i: (0, 0), **const)]
    args = [hlru, hlru, z, yb, yc_lat, yc_ctx, z, z, z, z, z, z, x, mod3, wbr_bf, wout_bf]
    if fuse:
        in_specs += [pl.BlockSpec((tm, d), lambda i: (i, 0)),
                     pl.BlockSpec((None, 1, d), lambda i: (mrow(i), 0, 5))]
        args += [moe, mod3_prev]
    return pl.pallas_call(
        functools.partial(_merge_kernel, lat_tiles, fuse),
        grid=(rows // tm,),
        in_specs=in_specs,
        out_specs=pl.BlockSpec((tm, d), lambda i: (i, 0)),
        out_shape=_sds((rows, d), F32),
        scratch_shapes=[pltpu.VMEM((tm, d), BF16)],
        compiler_params=_params(("parallel",)),
        name="merge_out_proj",
    )(*args)


def _router_kernel(x_ref, sh_ref, sc_ref, g_ref, rw_ref, rb_ref, h_ref, cls_ref, wt_ref):
    h = _modulate(x_ref[...], g_ref[...], sh_ref[...], sc_ref[...])
    h_ref[...] = h
    logits = lax.dot_general(rw_ref[...], h, (((1,), (1,)), ((), ())),
                             precision=lax.Precision.HIGHEST, preferred_element_type=F32)
    s = jax.nn.sigmoid(logits)
    sb = s + rb_ref[...]
    rows_b = [sb[e:e + 1, :] for e in range(N_GROUPS * EXP_PER_GROUP)]
    rows_s = [s[e:e + 1, :] for e in range(N_GROUPS * EXP_PER_GROUP)]
    gscore = []
    for g in range(N_GROUPS):
        v = rows_b[EXP_PER_GROUP * g:EXP_PER_GROUP * (g + 1)]
        best = None
        for a in range(EXP_PER_GROUP):
            for b in range(a + 1, EXP_PER_GROUP):
                pr = v[a] + v[b]
                best = pr if best is None else jnp.maximum(best, pr)
        gscore.append(best)
    best = gscore[0]
    gsel = jnp.zeros(best.shape, I32)
    for g in range(1, N_GROUPS):
        better = gscore[g] > best
        best = jnp.where(better, gscore[g], best)
        gsel = jnp.where(better, g, gsel)

    def pick(rows, i):
        out = rows[i]
        for g in range(1, N_GROUPS):
            out = jnp.where(gsel == g, rows[EXP_PER_GROUP * g + i], out)
        return out

    vb = [pick(rows_b, i) for i in range(EXP_PER_GROUP)]
    vs = [pick(rows_s, i) for i in range(EXP_PER_GROUP)]
    sel = []
    for i in range(EXP_PER_GROUP):
        rank = jnp.zeros(best.shape, I32)
        for jx in range(EXP_PER_GROUP):
            if jx == i:
                continue
            ahead = (vb[jx] > vb[i]) if jx > i else (vb[jx] >= vb[i])
            rank = rank + ahead.astype(I32)
        sel.append(rank < 2)
    j_hi = jnp.where(sel[3], 3, jnp.where(sel[2], 2, 1))
    w_lo = jnp.where(sel[0], vs[0], jnp.where(sel[1], vs[1], vs[2]))
    w_hi = jnp.where(sel[3], vs[3], jnp.where(sel[2], vs[2], vs[1]))
    den = w_lo + w_hi
    pair = jnp.where(sel[0], j_hi - 1, jnp.where(sel[1], j_hi + 1, 5))
    cls_ref[...] = gsel * N_PAIRS + pair
    wt_ref[0:1, :] = w_lo / den
    wt_ref[1:2, :] = w_hi / den


def _router_call(x, mod3, norm_g, rw_t, rb, rows, seq_len, n_batch):
    d = x.shape[1]
    e = rw_t.shape[0]
    tm = _tile(math.gcd(seq_len, rows), 512)

    def mrow(i):
        return jnp.minimum((i * tm) // seq_len, n_batch)

    return pl.pallas_call(
        _router_kernel,
        grid=(rows // tm,),
        in_specs=[pl.BlockSpec((tm, d), lambda i: (i, 0)),
                  pl.BlockSpec((None, 1, d), lambda i: (mrow(i), 0, 3)),
                  pl.BlockSpec((None, 1, d), lambda i: (mrow(i), 0, 4)),
                  pl.BlockSpec((1, d), lambda i: (0, 0)),
                  pl.BlockSpec((e, d), lambda i: (0, 0)),
                  pl.BlockSpec((e, 1), lambda i: (0, 0))],
        out_specs=[pl.BlockSpec((tm, d), lambda i: (i, 0)),
                   pl.BlockSpec((1, tm), lambda i: (0, i)),
                   pl.BlockSpec((2, tm), lambda i: (0, i))],
        out_shape=[_sds((rows, d), F32), _sds((1, rows), I32), _sds((2, rows), F32)],
        compiler_params=_params(("parallel",)),
        name="moe_router",
    )(x, mod3, mod3, norm_g.reshape(1, d), rw_t, rb.reshape(e, 1))


_PAIR_LO = np.array([0, 0, 0, 1, 1, 2], np.int32)
_PAIR_HI = np.array([1, 2, 3, 2, 3, 3], np.int32)
_CLS_LO = np.concatenate([EXP_PER_GROUP * g + _PAIR_LO for g in range(N_GROUPS)])
_CLS_HI = np.concatenate([EXP_PER_GROUP * g + _PAIR_HI for g in range(N_GROUPS)])


def _moe_plan(cls, wts, tm):
    t = cls.shape[0]
    n_tiles = t // tm + N_CLASSES
    r = n_tiles * tm
    cnt = jnp.sum((cls[:, None] == jnp.arange(N_CLASSES, dtype=I32)[None, :]).astype(I32), axis=0)
    nt_c = (cnt + tm - 1) // tm
    t_end = jnp.cumsum(nt_c)
    t_off = t_end - nt_c
    tile = jnp.arange(n_tiles, dtype=I32)
    total = t_end[-1]
    valid = (tile < total).astype(I32)
    tcls = jnp.sum((t_end[None, :] <= jnp.minimum(tile, total - 1)[:, None]).astype(I32), axis=1)
    pad_end = jnp.cumsum(nt_c * tm - cnt)
    pcls = jnp.sum((pad_end[None, :] <= jnp.arange(r - t, dtype=I32)[:, None]).astype(I32), axis=1)
    keys = jnp.concatenate([2 * cls, 2 * pcls + 1])
    tok = jnp.concatenate([jnp.arange(t, dtype=I32), jnp.full((r - t,), -1, I32)])
    wpad = jnp.zeros((r - t,), F32)
    _, dst, w0, w1 = lax.sort((keys, tok, jnp.concatenate([wts[0], wpad]), jnp.concatenate([wts[1], wpad])),
                              num_keys=1, is_stable=True)
    src = jnp.maximum(dst, 0)
    wsort = jnp.stack([w0, w1], axis=1)
    par = jnp.where(valid == 1, (tile - t_off[tcls]) % 2, 0)
    lo = jnp.asarray(_CLS_LO)[tcls]
    hi = jnp.asarray(_CLS_HI)[tcls]
    e0 = jnp.where(par == 0, lo, hi)
    e1 = jnp.where(par == 0, hi, lo)
    step_e = jnp.stack([e0, e1], axis=1).reshape(-1)
    e_last = step_e[2 * (total - 1) + 1]
    step_e = jnp.where(jnp.repeat(valid, 2) == 1, step_e, e_last)
    return {"src": src.reshape(n_tiles, 1, tm), "dst": dst.reshape(n_tiles, 1, tm), "w": wsort,
            "step_e": step_e, "valid": valid, "flip": par, "n_tiles": n_tiles}


def _expert_kernel(tm, se_ref, va_ref, fl_ref, src_ref, dst_ref, wt_ref, h_hbm, w1_ref, w3_ref, w2_ref,
                   o_hbm, xg_ref, xb_ref, acc_ref, gsem, ssem):
    del se_ref
    i = pl.program_id(0)
    k = pl.program_id(1)

    def row_in(r):
        return pltpu.make_async_copy(h_hbm.at[pl.ds(src_ref[0, r], 1), :], xg_ref.at[pl.ds(r, 1), :], gsem)

    def row_out(r):
        return pltpu.make_async_copy(acc_ref.at[pl.ds(r, 1), :], o_hbm.at[pl.ds(dst_ref[0, r], 1), :], ssem)

    @pl.when(va_ref[i] == 1)
    def _():
        @pl.when(k == 0)
        def _():
            def start(r, c):
                row_in(r).start()
                return c
            lax.fori_loop(0, tm, start, 0, unroll=DMA_UNROLL)

            def wait(r, c):
                row_in(r).wait()
                return c
            lax.fori_loop(0, tm, wait, 0, unroll=DMA_UNROLL)
            xb_ref[...] = xg_ref[...].astype(BF16)

        x = xb_ref[...]
        h1 = jnp.dot(x, w1_ref[...], preferred_element_type=F32)
        h3 = jnp.dot(x, w3_ref[...], preferred_element_type=F32)
        a = ((h1 * jax.nn.sigmoid(h1)) * h3).astype(BF16)
        y = jnp.dot(a, w2_ref[...], preferred_element_type=F32)
        wsel = (k + fl_ref[i]) % 2
        wt = jnp.where(wsel == 0, wt_ref[:, 0:1], wt_ref[:, 1:2])

        @pl.when(k == 0)
        def _():
            acc_ref[...] = wt * y

        @pl.when(k == 1)
        def _():
            acc_ref[...] = acc_ref[...] + wt * y

            full = dst_ref[0, tm - 1] >= 0

            @pl.when(full)
            def _():
                def start(r, c):
                    row_out(r).start()
                    return c
                lax.fori_loop(0, tm, start, 0, unroll=DMA_UNROLL)

                def wait(r, c):
                    row_out(r).wait()
                    return c
                lax.fori_loop(0, tm, wait, 0, unroll=DMA_UNROLL)

            @pl.when(jnp.logical_not(full))
            def _():
                def start(r, c):
                    @pl.when(dst_ref[0, r] >= 0)
                    def _():
                        row_out(r).start()
                    return c
                lax.fori_loop(0, tm, start, 0, unroll=DMA_UNROLL)

                def wait(r, c):
                    @pl.when(dst_ref[0, r] >= 0)
                    def _():
                        row_out(r).wait()
                    return c
                lax.fori_loop(0, tm, wait, 0, unroll=DMA_UNROLL)


def _expert_call(h2, plan, w1_bf, w3_bf, w2_bf, tm):
    rows, d = h2.shape
    de = w1_bf.shape[2]
    n_tiles = plan["n_tiles"]
    grid_spec = pltpu.PrefetchScalarGridSpec(
        num_scalar_prefetch=3,
        grid=(n_tiles, 2),
        in_specs=[pl.BlockSpec((None, 1, tm), lambda i, k, se, va, fl: (i, 0, 0), memory_space=pltpu.SMEM),
                  pl.BlockSpec((None, 1, tm), lambda i, k, se, va, fl: (i, 0, 0), memory_space=pltpu.SMEM),
                  pl.BlockSpec((tm, 2), lambda i, k, se, va, fl: (i, 0)),
                  pl.BlockSpec(memory_space=pl.ANY),
                  pl.BlockSpec((None, d, de), lambda i, k, se, va, fl: (se[2 * i + k], 0, 0)),
                  pl.BlockSpec((None, d, de), lambda i, k, se, va, fl: (se[2 * i + k], 0, 0)),
                  pl.BlockSpec((None, de, d), lambda i, k, se, va, fl: (se[2 * i + k], 0, 0))],
        out_specs=pl.BlockSpec(memory_space=pl.ANY),
        scratch_shapes=[pltpu.VMEM((tm, d), F32), pltpu.VMEM((tm, d), BF16), pltpu.VMEM((tm, d), F32),
                        pltpu.SemaphoreType.DMA(()), pltpu.SemaphoreType.DMA(())])
    return pl.pallas_call(
        functools.partial(_expert_kernel, tm),
        grid_spec=grid_spec,
        out_shape=_sds((rows, d), F32),
        compiler_params=pltpu.CompilerParams(dimension_semantics=("arbitrary", "arbitrary"),
                                             vmem_limit_bytes=V7X_VMEM_LIMIT, disable_bounds_checks=True),
        name="moe_experts",
    )(plan["step_e"], plan["valid"], plan["flip"], plan["src"], plan["dst"], plan["w"], h2,
      w1_bf, w3_bf, w2_bf)


def _final_kernel(x_ref, moe_ref, g2_ref, g_ref, o_ref):
    x = x_ref[...] + g2_ref[...] * moe_ref[...]
    ms = jnp.mean(x * x, axis=-1, keepdims=True)
    o_ref[...] = x * lax.rsqrt(ms + EPS) * g_ref[...]


def _final_call(x, moe, mod3, final_g, rows, seq_len, n_batch):
    d = x.shape[1]
    tm = _tile(math.gcd(seq_len, rows), 512)
    return pl.pallas_call(
        _final_kernel,
        grid=(rows // tm,),
        in_specs=[pl.BlockSpec((tm, d), lambda i: (i, 0)),
                  pl.BlockSpec((tm, d), lambda i: (i, 0)),
                  pl.BlockSpec((None, 1, d), lambda i: (jnp.minimum((i * tm) // seq_len, n_batch), 0, 5)),
                  pl.BlockSpec((1, d), lambda i: (0, 0))],
        out_specs=pl.BlockSpec((tm, d), lambda i: (i, 0)),
        out_shape=_sds((rows, d), F32),
        compiler_params=_params(("parallel",)),
        name="final_norm",
    )(x, moe, mod3, final_g.reshape(1, d))


def kernel(x, c, ctx, c_ctx, ada_w, ada_b, norm1_g, norm2_g, w_in, conv_a_w, conv_a_b, lru_wa, lru_ba, lru_wx, lru_bx, lru_lam, sgu_norm_g, sgu_ws, sgu_bs, hy_conv_w, hy_conv_b, hy_w1, hy_b1, hy_w2, hy_b2, hy_w3, hy_freq, hy_skip, w_br, w_out, exp_w1, exp_w3, exp_w2, router_w, router_b, final_g):
    n_batch, n_lat, d = x.shape
    n_ctx = ctx.shape[1]
    depth = ada_w.shape[0]
    w = d // 2
    t_lat = n_batch * n_lat
    t_ctx = n_batch * n_ctx
    t_all = t_lat + t_ctx
    heads, bw = lru_wa.shape[2], lru_wa.shape[3]
    groups = sgu_ws.shape[1]
    emb = hy_w1.shape[1]
    assert n_batch < MOD_ROWS and router_w.shape[1] == N_GROUPS * EXP_PER_GROUP
    assert n_lat % SGU_CHUNK == 0 and n_ctx % SGU_CHUNK == 0

    xs = jnp.concatenate([x.reshape(t_lat, d), ctx.reshape(t_ctx, d)], axis=0)
    c_all = jnp.zeros((MOD_ROWS, d), F32).at[:n_batch].set(c).at[n_batch].set(c_ctx)
    mod = _mod_all(c_all, ada_w, ada_b)

    consts_lat = _dft_consts(n_lat, w, emb)
    consts_ctx = _dft_consts(n_ctx, w, emb)
    rw_t = router_w.T
    tm_e = _tile(math.gcd(t_lat, t_ctx), MOE_TILE)

    moe = None
    mod3_prev = None
    for l in range(depth):
        last = l == depth - 1
        rows = t_lat if last else t_all
        mod3 = mod[l].reshape(MOD_ROWS, 1, 6 * d)
        z = _win_call(xs, moe, mod3, mod3_prev, norm1_g[l], w_in[l].astype(BF16), t_all, n_lat, n_batch)

        lw = {"conv_w": conv_a_w[l], "conv_b": conv_a_b[l].reshape(1, w),
              "wg": jnp.concatenate([lru_wa[l], lru_wx[l]], axis=-1).astype(BF16),
              "bg": jnp.concatenate([lru_ba[l].reshape(2, heads, 1, bw), lru_bx[l].reshape(2, heads, 1, bw)], axis=-1),
              "lam": lru_lam[l].reshape(2, 1, w)}
        hlru = _lru_call(z, lw, n_lat, n_ctx, n_batch)

        bias_full = jnp.repeat(sgu_bs[l].T, w // groups, axis=1)
        yb = _sgu_call(z, sgu_norm_g[l], sgu_ws[l].astype(BF16), bias_full, rows)

        hw = {"conv_w": hy_conv_w[l], "conv_b": hy_conv_b[l],
              "w1": jnp.pad(hy_w1[l], ((0, 128 - emb), (0, 0))), "b1": hy_b1[l][None, :],
              "w2": hy_w2[l], "b2": hy_b2[l][None, :], "w3": hy_w3[l], "freq": hy_freq[l][None, :],
              "skip3": hy_skip[l].reshape(hy_skip.shape[1], 1, w)}
        yc_lat = _hyena_call(z, consts_lat, hw, 0, n_lat, n_batch)
        yc_ctx = yc_lat if last else _hyena_call(z, consts_ctx, hw, t_lat, n_ctx, n_batch)

        xs = _merge_call(hlru, z, yb, yc_lat, yc_ctx, xs, moe, mod3, mod3_prev, w_br[l].astype(BF16),
                         w_out[l].astype(BF16), rows, n_lat, n_batch)

        h2, cls, wts = _router_call(xs, mod3, norm2_g[l], rw_t, router_b, rows, n_lat, n_batch)
        plan = _moe_plan(cls[0], wts, tm_e)
        moe = _expert_call(h2, plan, exp_w1[l].astype(BF16), exp_w3[l].astype(BF16),
                           exp_w2[l].astype(BF16), tm_e)
        mod3_prev = mod3

    out = _final_call(xs, moe, mod3_prev, final_g, t_lat, n_lat, n_batch)
    return out.reshape(n_batch, n_lat, d)
```
